```python
import math
import numpy as np
import jax
import jax.numpy as jnp
from jax import lax

D_MODEL = 1024
BATCH = 32
SEQ = 256
DEPTH = 2
DEC_BATCH = 2
DEC_SEQ = 1024
PAST_LEN = 256

GRID_W = 64
HA = 4
DA = 64
HB = 4
DKB = 64
DVB = 128
RET_CHUNK = 64
HC = 4
DC = 128
NA_ROWS = 8
NA_COLS = 16
N_BRANCH = 3
BRANCH_W = 512
PEER_HEADS = 8
PEER_NKEYS = 128
PEER_DKEY = 256
PEER_TOPK = 16
PEER_N = PEER_NKEYS * PEER_NKEYS
PEER_TOKEN_BLOCK = 128
Q_BLOCK = 128
ROPE_BASE = 10000.0
EPS = 1e-6
IN_WIDTHS = (HA * 2 * DA, HA * 2 * DA, HA * 2 * DA, HB * DKB, HB * DKB, HB * DVB, 2 * HB * DVB,
             HC * DC, HC * DC, HC * DC, N_BRANCH * D_MODEL)
IN_TOTAL = sum(IN_WIDTHS)

kernel_name = "hybrid_diffusion_prefix_trunk"


def rms_norm(x, g):
    x32 = x.astype(jnp.float32)
    y = x32 * lax.rsqrt(jnp.mean(x32 * x32, axis=-1, keepdims=True) + EPS)
    return (y * g.astype(jnp.float32)).astype(x.dtype)


def split_heads(x, h):
    b, L, _ = x.shape
    return x.reshape(b, L, h, -1).transpose(0, 2, 1, 3)


def merge_heads(x):
    b, h, L, d = x.shape
    return x.transpose(0, 2, 1, 3).reshape(b, L, h * d)


def rotate(x, ang):
    x1, x2 = jnp.split(x, 2, axis=-1)
    cos = jnp.cos(ang).astype(x.dtype)
    sin = jnp.sin(ang).astype(x.dtype)
    return jnp.concatenate([x1 * cos - x2 * sin, x1 * sin + x2 * cos], axis=-1)


def rope_2d(x):
    L, d = x.shape[-2], x.shape[-1]
    half = d // 2
    t = jnp.arange(L)
    rows = (t // GRID_W).astype(jnp.float32)
    cols = (t % GRID_W).astype(jnp.float32)
    freqs = ROPE_BASE ** (-jnp.arange(0, half, 2, dtype=jnp.float32) / half)
    return jnp.concatenate([rotate(x[..., :half], rows[:, None] * freqs),
                            rotate(x[..., half:], cols[:, None] * freqs)], axis=-1)


def softmax_attn(q, k, v):
    b, h, lq, d = q.shape
    nb = lq // Q_BLOCK
    qb = q.reshape(b, h, nb, Q_BLOCK, d).transpose(2, 0, 1, 3, 4)
    scale = d ** -0.5

    def one(qi):
        s = jnp.einsum('bhqd,bhkd->bhqk', qi, k).astype(jnp.float32) * scale
        p = jax.nn.softmax(s, axis=-1).astype(v.dtype)
        return jnp.einsum('bhqk,bhkd->bhqd', p, v)

    o = lax.map(one, qb)
    return o.transpose(1, 2, 0, 3, 4).reshape(b, h, lq, v.shape[-1])


def neighbourhood_attn(q, k, v, kc, vc, bias_table):
    b, h, L, d = q.shape
    rows = L // GRID_W
    kr = min(NA_ROWS, rows)
    scale = d ** -0.5
    qg = q.reshape(b, h, rows, GRID_W, d)
    kg = k.reshape(b, h, rows, GRID_W, d)
    vg = v.reshape(b, h, rows, GRID_W, -1)
    r = np.arange(rows)
    rs = np.clip(r - kr // 2, 0, rows - kr)
    row_idx = rs[:, None] + np.arange(kr)[None, :]
    kb = kg[:, :, row_idx]
    vb = vg[:, :, row_idx]
    cq = np.arange(GRID_W)
    cs = np.clip(cq - NA_COLS // 2, 0, GRID_W - NA_COLS)
    col_ok = (cq[None, :] >= cs[:, None]) & (cq[None, :] < cs[:, None] + NA_COLS)
    roff = row_idx - r[:, None] + (NA_ROWS - 1)
    coff = np.clip(cq[None, :] - cq[:, None] + (NA_COLS - 1), 0, 2 * NA_COLS - 2)
    bias = bias_table[:, roff[:, :, None, None], coff[None, None, :, :]]
    bias = bias.transpose(0, 1, 3, 2, 4).astype(jnp.float32)
    s = jnp.einsum('bhrqd,bhrjkd->bhrqjk', qg, kb).astype(jnp.float32) * scale + bias[None]
    s = jnp.where(col_ok[None, None, None, :, None, :], s, -jnp.inf)
    s = s.reshape(b, h, rows, GRID_W, kr * GRID_W)
    sc = jnp.einsum('bhrqd,bhkd->bhrqk', qg, kc).astype(jnp.float32) * scale
    p = jax.nn.softmax(jnp.concatenate([s, sc], axis=-1), axis=-1).astype(v.dtype)
    pl = p[..., :kr * GRID_W].reshape(b, h, rows, GRID_W, kr, GRID_W)
    pc = p[..., kr * GRID_W:]
    o = jnp.einsum('bhrqjk,bhrjkd->bhrqd', pl, vb) + jnp.einsum('bhrqk,bhkd->bhrqd', pc, vc)
    return o.reshape(b, h, L, -1)


def retention_chunked(q, k, v, log_gamma, s0):
    b, h, L, dk = q.shape
    dv = v.shape[-1]
    C = RET_CHUNK
    n = L // C
    k = k * (dk ** -0.5)
    pos = jnp.arange(C, dtype=jnp.float32)
    lg = log_gamma.astype(jnp.float32)[:, None]
    diff = pos[:, None] - pos[None, :]
    intra = jnp.where(diff >= 0, jnp.exp(lg[:, :, None] * jnp.maximum(diff, 0.0)), 0.0).astype(q.dtype)
    q_dec = jnp.exp(lg * (pos + 1.0)).astype(q.dtype)[None, :, :, None]
    k_dec = jnp.exp(lg * (C - 1.0 - pos)).astype(q.dtype)[None, :, :, None]
    c_dec = jnp.exp(lg * float(C)).astype(q.dtype)[None, :, :, None]

    def chunks(x):
        return x.reshape(b, h, n, C, x.shape[-1]).transpose(2, 0, 1, 3, 4)

    def step(S, inp):
        qi, ki, vi = inp
        inner = jnp.einsum('bhqd,bhkd->bhqk', qi, ki) * intra[None]
        o = jnp.einsum('bhqk,bhkd->bhqd', inner, vi) + jnp.einsum('bhqd,bhde->bhqe', qi * q_dec, S)
        S = S * c_dec + jnp.einsum('bhkd,bhke->bhde', ki * k_dec, vi)
        return S, o

    S, o = lax.scan(step, s0, (chunks(q), chunks(k), chunks(v)))
    return o.transpose(1, 2, 0, 3, 4).reshape(b, h, L, dv), S


def bidir_retention(q, k, v, gates, log_gammas, s0f, s0b, gn_gain):
    of, sf = retention_chunked(q, k, v, log_gammas[0], s0f)
    ob, sb = retention_chunked(jnp.flip(q, 2), jnp.flip(k, 2), jnp.flip(v, 2), log_gammas[1], s0b)
    ob = jnp.flip(ob, 2)
    gf, gb = jnp.split(gates, 2, axis=-1)
    y = jax.nn.silu(gf) * merge_heads(rms_norm(of, gn_gain)) + jax.nn.silu(gb) * merge_heads(rms_norm(ob, gn_gain))
    return y, jnp.stack([sf, sb], axis=1)


def lambda_value(lam_l, lam_init):
    lam_l = lam_l.astype(jnp.float32)
    lq1, lk1, lq2, lk2 = lam_l[0], lam_l[1], lam_l[2], lam_l[3]
    return jnp.exp(jnp.sum(lq1 * lk1)) - jnp.exp(jnp.sum(lq2 * lk2)) + lam_init


def peer(h, wq, subkeys, u_tab, v_tab):
    b, L, D = h.shape
    nb = (b * L) // PEER_TOKEN_BLOCK
    hb = h.reshape(nb, PEER_TOKEN_BLOCK, D)

    def one(xt):
        t = xt.shape[0]
        q = (xt @ wq).reshape(t, PEER_HEADS, 2, PEER_DKEY // 2)
        s1 = jnp.einsum('thd,nd->thn', q[:, :, 0], subkeys[0]).astype(jnp.float32)
        s2 = jnp.einsum('thd,nd->thn', q[:, :, 1], subkeys[1]).astype(jnp.float32)
        v1, i1 = lax.top_k(s1, PEER_TOPK)
        v2, i2 = lax.top_k(s2, PEER_TOPK)
        cand = (v1[..., :, None] + v2[..., None, :]).reshape(t, PEER_HEADS, PEER_TOPK * PEER_TOPK)
        cidx = (i1[..., :, None] * PEER_NKEYS + i2[..., None, :]).reshape(t, PEER_HEADS, PEER_TOPK * PEER_TOPK)
        sc, pos = lax.top_k(cand, PEER_TOPK)
        ids = jnp.take_along_axis(cidx, pos, axis=-1)
        g = jax.nn.softmax(sc, axis=-1).astype(xt.dtype)
        ue = u_tab[ids]
        ve = v_tab[ids]
        a = jax.nn.gelu(jnp.einsum('thkd,td->thk', ue, xt))
        return jnp.einsum('thk,thkd->td', g * a, ve)

    return lax.map(one, hb).reshape(b, L, D)


def modulation(cond, w_mod_l, b_mod_l):
    m = jax.nn.silu(cond) @ w_mod_l + b_mod_l
    return jnp.split(m[:, None, :], 6, axis=-1)


def split_proj(p):
    return jnp.split(p, np.cumsum(IN_WIDTHS)[:-1].tolist(), axis=-1)


def merge_out(mg, ya, yb, yc, wbr_l, wout_l):
    g0, g1, g2 = jnp.split(jax.nn.sigmoid(mg), 3, axis=-1)
    y = g0 * (ya @ wbr_l[0]) + g1 * (yb @ wbr_l[1]) + g2 * (yc @ wbr_l[2])
    return y @ wout_l


def trunk_layer(x, cond, cache, w_mod_l, b_mod_l, norm1_l, w_in_l, lam_l, dgn_l, decay_l, rgn_l, nab_l,
                wbr_l, wout_l, norm2_l, pwq_l, psk_l, pu_l, pv_l, lam_init):
    b = x.shape[0]
    sh1, sc1, g1, sh2, sc2, g2 = modulation(cond, w_mod_l, b_mod_l)
    h = rms_norm(x, norm1_l) * (1 + sc1) + sh1
    qa, ka, va, qb, kb, vb, gb, qc, kc, vc, mg = split_proj(h @ w_in_l)
    qa, ka, va = split_heads(qa, HA), split_heads(ka, HA), split_heads(va, HA)
    qb, kb, vb = split_heads(qb, HB), split_heads(kb, HB), split_heads(vb, HB)
    qc, kc, vc = split_heads(qc, HC), split_heads(kc, HC), split_heads(vc, HC)
    lam = lambda_value(lam_l, lam_init).astype(x.dtype)
    log_gammas = jax.nn.log_sigmoid(decay_l.astype(jnp.float32))
    if cache is None:
        q1, q2 = qa[..., :DA], qa[..., DA:]
        k1, k2 = ka[..., :DA], ka[..., DA:]
        vav = va
        s0f = jnp.zeros((b, HB, DKB, DVB), x.dtype)
        s0b = jnp.zeros((b, HB, DKB, DVB), x.dtype)
        oc = softmax_attn(qc, kc, vc)
    else:
        c_dk, c_dv, c_nk, c_nv, c_st = cache
        q1, q2 = rope_2d(qa[..., :DA]), rope_2d(qa[..., DA:])
        k1 = jnp.concatenate([c_dk[..., :DA], rope_2d(ka[..., :DA])], axis=2)
        k2 = jnp.concatenate([c_dk[..., DA:], rope_2d(ka[..., DA:])], axis=2)
        vav = jnp.concatenate([c_dv, va], axis=2)
        qb, kb = rope_2d(qb), rope_2d(kb)
        s0f, s0b = c_st[:, 0], c_st[:, 1]
        oc = neighbourhood_attn(qc, kc, vc, c_nk, c_nv, nab_l)
    o_diff = softmax_attn(q1, k1, vav) - lam * softmax_attn(q2, k2, vav)
    oa = rms_norm(o_diff, dgn_l) * (1.0 - lam_init)
    ob, s_ret = bidir_retention(qb, kb, vb, gb, log_gammas, s0f, s0b, rgn_l)
    x = x + g1 * merge_out(mg, merge_heads(oa), ob, merge_heads(oc), wbr_l, wout_l)
    h2 = rms_norm(x, norm2_l) * (1 + sc2) + sh2
    x = x + g2 * peer(h2, pwq_l, psk_l, pu_l, pv_l)
    new_ctx = (ka, va, kc, vc, s_ret) if cache is None else None
    return x, new_ctx


def setup_inputs(seed: int = 0) -> dict:
    key = jax.random.key(seed)
    ks = jax.random.split(key, 26)
    f32 = jnp.float32
    nrm = lambda k, s, sc=1.0: jax.random.normal(k, s, f32) * sc
    dec_logit = jnp.log(2.0 ** (5.0 + jnp.arange(HB, dtype=f32)) - 1.0)
    return {
        "x_prompt": nrm(ks[0], (BATCH, SEQ, D_MODEL)),
        "x_sample": nrm(ks[1], (DEC_BATCH, DEC_SEQ, D_MODEL)),
        "c": nrm(ks[2], (DEC_BATCH, D_MODEL)),
        "cache_diff_k": nrm(ks[3], (DEC_BATCH, DEPTH, HA, PAST_LEN, 2 * DA)),
        "cache_diff_v": nrm(ks[4], (DEC_BATCH, DEPTH, HA, PAST_LEN, 2 * DA)),
        "cache_na_k": nrm(ks[5], (DEC_BATCH, DEPTH, HC, PAST_LEN, DC)),
        "cache_na_v": nrm(ks[6], (DEC_BATCH, DEPTH, HC, PAST_LEN, DC)),
        "state_ret": nrm(ks[7], (DEC_BATCH, DEPTH, 2, HB, DKB, DVB)),
        "c_ctx": nrm(ks[8], (D_MODEL,)),
        "w_mod": nrm(ks[9], (DEPTH, D_MODEL, 6 * D_MODEL), 0.5 * D_MODEL ** -0.5),
        "b_mod": nrm(ks[10], (DEPTH, 6 * D_MODEL), 0.01),
        "norm1": 1.0 + nrm(ks[11], (DEPTH, D_MODEL), 0.01),
        "w_in": nrm(ks[12], (DEPTH, D_MODEL, IN_TOTAL), D_MODEL ** -0.5),
        "diff_lambda": nrm(ks[13], (DEPTH, 4, DA), 0.1),
        "diff_gn": 1.0 + nrm(ks[14], (DEPTH, 2 * DA), 0.01),
        "ret_decay": dec_logit + nrm(ks[15], (DEPTH, 2, HB), 0.1),
        "ret_gn": 1.0 + nrm(ks[16], (DEPTH, DVB), 0.01),
        "na_bias": nrm(ks[17], (DEPTH, HC, 2 * NA_ROWS - 1, 2 * NA_COLS - 1), 0.02),
        "w_branch": nrm(ks[18], (DEPTH, N_BRANCH, BRANCH_W, D_MODEL), BRANCH_W ** -0.5),
        "w_out": nrm(ks[19], (DEPTH, D_MODEL, D_MODEL), D_MODEL ** -0.5),
        "norm2": 1.0 + nrm(ks[20], (DEPTH, D_MODEL), 0.01),
        "peer_wq": nrm(ks[21], (DEPTH, D_MODEL, PEER_HEADS * PEER_DKEY), D_MODEL ** -0.5),
        "peer_subkeys": nrm(ks[22], (DEPTH, 2, PEER_NKEYS, PEER_DKEY // 2), (PEER_DKEY // 2) ** -0.5),
        "peer_u": nrm(ks[23], (DEPTH, PEER_N, D_MODEL), D_MODEL ** -0.5),
        "peer_v": nrm(ks[24], (DEPTH, PEER_N, D_MODEL), PEER_HEADS ** -0.5),
        "norm_f": 1.0 + nrm(ks[25], (D_MODEL,), 0.01),
    }


def reference(x_prompt, x_sample, c, cache_diff_k, cache_diff_v, cache_na_k, cache_na_v, state_ret,
              c_ctx, w_mod, b_mod, norm1, w_in, diff_lambda, diff_gn, ret_decay, ret_gn, na_bias,
              w_branch, w_out, norm2, peer_wq, peer_subkeys, peer_u, peer_v, norm_f):
    xp = x_prompt
    cond_ctx = c_ctx[None, :]
    kd, vd, kn, vn, sr = [], [], [], [], []
    for l in range(DEPTH):
        lam_init = 0.8 - 0.6 * math.exp(-0.3 * l)
        xp, ctx = trunk_layer(xp, cond_ctx, None, w_mod[l], b_mod[l], norm1[l], w_in[l], diff_lambda[l],
                              diff_gn[l], ret_decay[l], ret_gn[l], na_bias[l], w_branch[l], w_out[l],
                              norm2[l], peer_wq[l], peer_subkeys[l], peer_u[l], peer_v[l], lam_init)
        kd.append(ctx[0])
        vd.append(ctx[1])
        kn.append(ctx[2])
        vn.append(ctx[3])
        sr.append(ctx[4])
    y_prompt = rms_norm(xp, norm_f)
    new_cache_diff_k = jnp.stack(kd, axis=1)
    new_cache_diff_v = jnp.stack(vd, axis=1)
    new_cache_na_k = jnp.stack(kn, axis=1)
    new_cache_na_v = jnp.stack(vn, axis=1)
    new_state_ret = jnp.stack(sr, axis=1)

    xs = x_sample
    for l in range(DEPTH):
        lam_init = 0.8 - 0.6 * math.exp(-0.3 * l)
        cache = (cache_diff_k[:, l], cache_diff_v[:, l], cache_na_k[:, l], cache_na_v[:, l], state_ret[:, l])
        xs, _ = trunk_layer(xs, c, cache, w_mod[l], b_mod[l], norm1[l], w_in[l], diff_lambda[l],
                            diff_gn[l], ret_decay[l], ret_gn[l], na_bias[l], w_branch[l], w_out[l],
                            norm2[l], peer_wq[l], peer_subkeys[l], peer_u[l], peer_v[l], lam_init)
    y_sample = rms_norm(xs, norm_f)
    return (y_prompt, y_sample, new_cache_diff_k, new_cache_diff_v, new_cache_na_k, new_cache_na_v, new_state_ret)
```

```python
import functools
import math

import numpy as np
import jax
import jax.numpy as jnp
from jax import lax
from jax.experimental import pallas as pl
from jax.experimental.pallas import tpu as pltpu

D_MODEL = 1024
DEPTH = 2
GRID_W = 64
HA = 4
DA = 64
HB = 4
DKB = 64
DVB = 128
HC = 4
DC = 128
NA_ROWS = 8
NA_COLS = 16
PEER_HEADS = 8
PEER_NKEYS = 128
PEER_DKEY = 256
PEER_TOPK = 16
PEER_N = PEER_NKEYS * PEER_NKEYS
ROPE_BASE = 10000.0
EPS = 1e-6

_QA, _KA, _VA = 0, 512, 1024
_QB, _KB, _VB, _GF, _GB = 1536, 1792, 2048, 2560, 3072
_QC, _KC, _VC, _MG = 3584, 4096, 4608, 5120
IN_TOTAL = 8192

LANES = 128
SUBLANES = 8
MASK_VALUE = -1e30
_WPITCH = 136

f32 = jnp.float32
bf16 = jnp.bfloat16


def _dot(a, b):
    return jnp.dot(a, b, preferred_element_type=f32)


def _dot_nt(a, b):
    return lax.dot_general(a, b, (((1,), (1,)), ((), ())), preferred_element_type=f32)


def _dot_tn(a, b):
    return lax.dot_general(a, b, (((0,), (0,)), ((), ())), preferred_element_type=f32)


def _sigmoid(x):
    return 1.0 / (1.0 + jnp.exp(-x))


def _silu(x):
    return x * _sigmoid(x)


def _rms(x, g):
    return x * lax.rsqrt(jnp.mean(x * x, axis=-1, keepdims=True) + EPS) * g


def _softmax(s):
    e = jnp.exp(s - jnp.max(s, axis=-1, keepdims=True))
    return e * (1.0 / jnp.sum(e, axis=-1, keepdims=True))


def _rope(x, c, sa, sb):
    return x * c + pltpu.roll(x, 16, 1) * sa + pltpu.roll(x, LANES - 16, 1) * sb


def _params(*sem):
    return pltpu.CompilerParams(dimension_semantics=sem)


def _mod_kernel(c_ref, w_ref, b_ref, o_ref):
    a = _silu(c_ref[...]).astype(bf16)
    o_ref[...] = _dot(a, w_ref[...].astype(bf16)) + b_ref[...]


def _modulation(conds, w_mod, b_mod):
    nc = conds.shape[0]
    n = w_mod.shape[-1]
    tn = 1536
    return pl.pallas_call(
        _mod_kernel,
        out_shape=jax.ShapeDtypeStruct((DEPTH, nc, n), f32),
        grid=(DEPTH, n // tn),
        in_specs=[
            pl.BlockSpec((nc, D_MODEL), lambda l, j: (0, 0)),
            pl.BlockSpec((None, D_MODEL, tn), lambda l, j: (l, 0, j)),
            pl.BlockSpec((None, 1, tn), lambda l, j: (l, 0, j)),
        ],
        out_specs=pl.BlockSpec((None, nc, tn), lambda l, j: (l, 0, j)),
        compiler_params=_params("arbitrary", "arbitrary"),
        name="modulation",
    )(conds, w_mod, b_mod.reshape(DEPTH, 1, n))


def _nmm_kernel(x_ref, g_ref, sc_ref, sh_ref, w_ref, o_ref, *rest, emit_h):
    h_scr = rest[-1]

    @pl.when(pl.program_id(1) == 0)
    def _():
        h = _rms(x_ref[...], g_ref[...]) * (1.0 + sc_ref[...]) + sh_ref[...]
        h_scr[...] = h.astype(bf16)
        if emit_h:
            rest[0][...] = h.astype(bf16)

    o_ref[...] = _dot(h_scr[...], w_ref[...])


def _norm_mod_matmul(x, gain, sc, sh, w, rows_per_cond, *, tm, tn, emit_h=False):
    t, d = x.shape
    n = w.shape[1]
    cond_map = lambda i, j: ((i * tm) // rows_per_cond, 0, 0)
    out_shape = [jax.ShapeDtypeStruct((t, n), f32)]
    out_specs = [pl.BlockSpec((tm, tn), lambda i, j: (i, j))]
    if emit_h:
        out_shape.append(jax.ShapeDtypeStruct((t, d), bf16))
        out_specs.append(pl.BlockSpec((tm, d), lambda i, j: (i, 0)))
    return pl.pallas_call(
        functools.partial(_nmm_kernel, emit_h=emit_h),
        out_shape=out_shape,
        grid=(t // tm, n // tn),
        in_specs=[
            pl.BlockSpec((tm, d), lambda i, j: (i, 0)),
            pl.BlockSpec((1, d), lambda i, j: (0, 0)),
            pl.BlockSpec((None, 1, d), cond_map),
            pl.BlockSpec((None, 1, d), cond_map),
            pl.BlockSpec((d, tn), lambda i, j: (0, j)),
        ],
        out_specs=out_specs,
        scratch_shapes=[pltpu.VMEM((tm, d), bf16)],
        compiler_params=_params("arbitrary", "arbitrary"),
        name="norm_mod_matmul",
    )(x, gain.reshape(1, d), sc, sh, w)


def _lambda(lam_ref, lam_init):
    l = lam_ref[...]
    a = jnp.sum(l[0:1] * l[1:2], axis=-1, keepdims=True)
    b = jnp.sum(l[2:3] * l[3:4], axis=-1, keepdims=True)
    return jnp.exp(a) - jnp.exp(b) + lam_init


def _diff_core(q, kb, vb, lam_ref, gn_ref, lam_init):
    lane = lax.broadcasted_iota(jnp.int32, (1, LANES), 1)
    lo = lane < DA
    q1 = jnp.where(lo, q, 0.0).astype(bf16)
    q2 = jnp.where(lo, 0.0, q).astype(bf16)
    scale = DA ** -0.5
    p1 = _softmax(_dot_nt(q1, kb) * scale).astype(bf16)
    p2 = _softmax(_dot_nt(q2, kb) * scale).astype(bf16)
    o = _dot(p1, vb) - _lambda(lam_ref, lam_init) * _dot(p2, vb)
    return _rms(o, gn_ref[...]) * (1.0 - lam_init)


def _diff_ctx_kernel(q_ref, k_ref, v_ref, lam_ref, gn_ref, o_ref, ko_ref, vo_ref, *, lam_init):
    k = k_ref[...]
    v = v_ref[...]
    ko_ref[...] = k
    vo_ref[...] = v
    o_ref[...] = _diff_core(q_ref[...], k.astype(bf16), v.astype(bf16), lam_ref, gn_ref, lam_init)


def _diff_attn_ctx(p3, lam_l, gn_l, lam_init):
    b, l, _ = p3.shape
    col = lambda off: (lambda i, h: (i, 0, off // LANES + h))
    head_out = pl.BlockSpec((None, None, l, LANES), lambda i, h: (i, h, 0, 0))
    return pl.pallas_call(
        functools.partial(_diff_ctx_kernel, lam_init=lam_init),
        out_shape=[
            jax.ShapeDtypeStruct((b, l, HA * 2 * DA), f32),
            jax.ShapeDtypeStruct((b, HA, l, 2 * DA), f32),
            jax.ShapeDtypeStruct((b, HA, l, 2 * DA), f32),
        ],
        grid=(b, HA),
        in_specs=[
            pl.BlockSpec((None, l, LANES), col(_QA)),
            pl.BlockSpec((None, l, LANES), col(_KA)),
            pl.BlockSpec((None, l, LANES), col(_VA)),
            pl.BlockSpec((4, DA), lambda i, h: (0, 0)),
            pl.BlockSpec((1, 2 * DA), lambda i, h: (0, 0)),
        ],
        out_specs=[pl.BlockSpec((None, l, LANES), lambda i, h: (i, 0, h)), head_out, head_out],
        compiler_params=_params("arbitrary", "arbitrary"),
        name="diff_attn_ctx",
    )(p3, p3, p3, lam_l, gn_l.reshape(1, 2 * DA))


def _diff_lat_kernel(q_ref, k_ref, v_ref, ck_ref, cv_ref, cq, saq, sbq, ckk, sak, sbk,
                     lam_ref, gn_ref, o_ref, k_scr, v_scr, *, lam_init, past):
    @pl.when(pl.program_id(2) == 0)
    def _():
        k_scr[0:past, :] = ck_ref[...].astype(bf16)
        k_scr[past:, :] = _rope(k_ref[...], ckk[...], sak[...], sbk[...]).astype(bf16)
        v_scr[0:past, :] = cv_ref[...].astype(bf16)
        v_scr[past:, :] = v_ref[...].astype(bf16)

    q = _rope(q_ref[...], cq[...], saq[...], sbq[...])
    o_ref[...] = _diff_core(q, k_scr[...], v_scr[...], lam_ref, gn_ref, lam_init)


def _diff_attn_lat(p3, c_dk, c_dv, rope_tabs, lam_l, gn_l, lam_init, *, tq):
    b, l, _ = p3.shape
    past = c_dk.shape[2]
    col = lambda off: (lambda i, h, qi: (i, 0, off // LANES + h))
    cache = pl.BlockSpec((None, None, past, LANES), lambda i, h, qi: (i, h, 0, 0))
    tab_q = pl.BlockSpec((tq, LANES), lambda i, h, qi: (qi, 0))
    tab_k = pl.BlockSpec((l, LANES), lambda i, h, qi: (0, 0))
    return pl.pallas_call(
        functools.partial(_diff_lat_kernel, lam_init=lam_init, past=past),
        out_shape=jax.ShapeDtypeStruct((b, l, HA * 2 * DA), f32),
        grid=(b, HA, l // tq),
        in_specs=[
            pl.BlockSpec((None, tq, LANES), lambda i, h, qi: (i, qi, _QA // LANES + h)),
            pl.BlockSpec((None, l, LANES), col(_KA)),
            pl.BlockSpec((None, l, LANES), col(_VA)),
            cache, cache, tab_q, tab_q, tab_q, tab_k, tab_k, tab_k,
            pl.BlockSpec((4, DA), lambda i, h, qi: (0, 0)),
            pl.BlockSpec((1, 2 * DA), lambda i, h, qi: (0, 0)),
        ],
        out_specs=pl.BlockSpec((None, tq, LANES), lambda i, h, qi: (i, qi, h)),
        scratch_shapes=[pltpu.VMEM((past + l, LANES), bf16), pltpu.VMEM((past + l, LANES), bf16)],
        compiler_params=_params("arbitrary", "arbitrary", "arbitrary"),
        name="diff_attn_lat",
    )(p3, p3, p3, c_dk, c_dv, *rope_tabs, *rope_tabs, lam_l, gn_l.reshape(1, 2 * DA))


def _dense_attn_kernel(q_ref, k_ref, v_ref, o_ref, ko_ref, vo_ref):
    k = k_ref[...]
    v = v_ref[...]
    ko_ref[...] = k
    vo_ref[...] = v
    s = _dot_nt(q_ref[...].astype(bf16), k.astype(bf16)) * (DC ** -0.5)
    o_ref[...] = _dot(_softmax(s).astype(bf16), v.astype(bf16))


def _dense_attn_ctx(p3):
    b, l, _ = p3.shape
    col = lambda off: (lambda i, h: (i, 0, off // LANES + h))
    head_out = pl.BlockSpec((None, None, l, LANES), lambda i, h: (i, h, 0, 0))
    return pl.pallas_call(
        _dense_attn_kernel,
        out_shape=[
            jax.ShapeDtypeStruct((b, l, HC * DC), f32),
            jax.ShapeDtypeStruct((b, HC, l, DC), f32),
            jax.ShapeDtypeStruct((b, HC, l, DC), f32),
        ],
        grid=(b, HC),
        in_specs=[
            pl.BlockSpec((None, l, LANES), col(_QC)),
            pl.BlockSpec((None, l, LANES), col(_KC)),
            pl.BlockSpec((None, l, LANES), col(_VC)),
        ],
        out_specs=[pl.BlockSpec((None, l, LANES), lambda i, h: (i, 0, h)), head_out, head_out],
        compiler_params=_params("arbitrary", "arbitrary"),
        name="dense_attn_ctx",
    )(p3, p3, p3)


def _na_kernel(q_ref, k_ref, v_ref, ck_ref, cv_ref, bias_ref, o_ref, *, rows, kr):
    scale = DC ** -0.5
    ckb = ck_ref[...].astype(bf16)
    cvb = cv_ref[...].astype(bf16)
    for r in range(rows):
        rs = min(max(r - kr // 2, 0), rows - kr)
        qr = q_ref[r * GRID_W:(r + 1) * GRID_W, :].astype(bf16)
        kl = k_ref[rs * GRID_W:(rs + kr) * GRID_W, :].astype(bf16)
        vl = v_ref[rs * GRID_W:(rs + kr) * GRID_W, :].astype(bf16)
        sl = _dot_nt(qr, kl) * scale + bias_ref[r]
        sc = _dot_nt(qr, ckb) * scale
        m = jnp.maximum(jnp.max(sl, axis=-1, keepdims=True), jnp.max(sc, axis=-1, keepdims=True))
        el = jnp.exp(sl - m)
        ec = jnp.exp(sc - m)
        inv = 1.0 / (jnp.sum(el, axis=-1, keepdims=True) + jnp.sum(ec, axis=-1, keepdims=True))
        pl_ = (el * inv).astype(bf16)
        pc_ = (ec * inv).astype(bf16)
        o_ref[r * GRID_W:(r + 1) * GRID_W, :] = _dot(pl_, vl) + _dot(pc_, cvb)


def _na_bias(table, rows, kr):
    r = np.arange(rows)
    rs = np.clip(r - kr // 2, 0, rows - kr)
    row_idx = rs[:, None] + np.arange(kr)[None, :]
    cq = np.arange(GRID_W)
    cs = np.clip(cq - NA_COLS // 2, 0, GRID_W - NA_COLS)
    col_ok = (cq[None, :] >= cs[:, None]) & (cq[None, :] < cs[:, None] + NA_COLS)
    roff = row_idx - r[:, None] + (NA_ROWS - 1)
    coff = np.clip(cq[None, :] - cq[:, None] + (NA_COLS - 1), 0, 2 * NA_COLS - 2)
    bias = table[:, roff[:, :, None, None], coff[None, None, :, :]]
    bias = bias.transpose(0, 1, 3, 2, 4).astype(f32)
    bias = jnp.where(col_ok[None, None, :, None, :], bias, MASK_VALUE)
    return bias.reshape(table.shape[0], rows, GRID_W, kr * GRID_W)


def _na_attn_lat(p3, c_nk, c_nv, bias):
    b, l, _ = p3.shape
    past = c_nk.shape[2]
    rows = l // GRID_W
    kr = min(NA_ROWS, rows)
    col = lambda off: (lambda i, h: (i, 0, off // LANES + h))
    cache = pl.BlockSpec((None, None, past, LANES), lambda i, h: (i, h, 0, 0))
    return pl.pallas_call(
        functools.partial(_na_kernel, rows=rows, kr=kr),
        out_shape=jax.ShapeDtypeStruct((b, l, HC * DC), f32),
        grid=(b, HC),
        in_specs=[
            pl.BlockSpec((None, l, LANES), col(_QC)),
            pl.BlockSpec((None, l, LANES), col(_KC)),
            pl.BlockSpec((None, l, LANES), col(_VC)),
            cache, cache,
            pl.BlockSpec((None, rows, GRID_W, kr * GRID_W), lambda i, h: (h, 0, 0, 0)),
        ],
        out_specs=pl.BlockSpec((None, l, LANES), lambda i, h: (i, 0, h)),
        compiler_params=_params("arbitrary", "arbitrary"),
        name="na_attn_lat",
    )(p3, p3, p3, c_nk, c_nv, bias)


def _log_sigmoid(x):
    return -(jnp.maximum(-x, 0.0) + jnp.log1p(jnp.exp(-jnp.abs(x))))


def _ret_kernel(*refs, l, tq, rope, has_s0, emit_state):
    it = iter(refs)
    q_ref, k_ref, v_ref, gf_ref, gb_ref, dec_ref, gn_ref = (next(it) for _ in range(7))
    if rope:
        cq, saq, sbq, ckk, sak, sbk = (next(it) for _ in range(6))
    if has_s0:
        s0_ref = next(it)
    y_ref = next(it)
    if emit_state:
        s_ref = next(it)

    qi = pl.program_id(2)
    q = q_ref[...]
    k = k_ref[...]
    if rope:
        q = _rope(q, cq[...], saq[...], sbq[...])
        k = _rope(k, ckk[...], sak[...], sbk[...])
    k = k * (DKB ** -0.5)
    kb = k.astype(bf16)
    lane = lax.broadcasted_iota(jnp.int32, (1, LANES), 1)
    tpos = (qi * tq + lax.broadcasted_iota(jnp.int32, (tq, 1), 0)).astype(f32)
    spos_row = lax.broadcasted_iota(jnp.int32, (1, l), 1).astype(f32)
    diff = tpos - spos_row
    lg = _log_sigmoid(dec_ref[...])
    gn = gn_ref[...]
    for hh in range(2):
        mh = (lane >= DKB * hh) & (lane < DKB * (hh + 1))
        qh = jnp.where(mh, q, 0.0)
        qk = _dot_nt(qh.astype(bf16), kb)
        lgf = lg[0, hh][0:1, 0:1]
        lgb = lg[1, hh][0:1, 0:1]
        df = jnp.where(diff >= 0, jnp.exp(lgf * jnp.maximum(diff, 0.0)), 0.0)
        db = jnp.where(diff <= 0, jnp.exp(lgb * jnp.maximum(-diff, 0.0)), 0.0)
        vh = v_ref[:, hh * DVB:(hh + 1) * DVB].astype(bf16)
        of = _dot((qk * df).astype(bf16), vh)
        ob = _dot((qk * db).astype(bf16), vh)
        if has_s0:
            qf = (qh * jnp.exp(lgf * (tpos + 1.0))).astype(bf16)
            qb = (qh * jnp.exp(lgb * (float(l) - tpos))).astype(bf16)
            of = of + _dot(qf, s0_ref[0].reshape(2 * DKB, DVB).astype(bf16))
            ob = ob + _dot(qb, s0_ref[1].reshape(2 * DKB, DVB).astype(bf16))
        if emit_state:
            spos = lax.broadcasted_iota(jnp.int32, (l, 1), 0).astype(f32)
            kf = (k * jnp.exp(lgf * (float(l) - 1.0 - spos))).astype(bf16)
            kw = (k * jnp.exp(lgb * spos)).astype(bf16)
            s_ref[0, hh] = _dot_tn(kf, vh)[hh * DKB:(hh + 1) * DKB, :]
            s_ref[1, hh] = _dot_tn(kw, vh)[hh * DKB:(hh + 1) * DKB, :]
        sl = slice(hh * DVB, (hh + 1) * DVB)
        y_ref[:, sl] = _silu(gf_ref[:, sl]) * _rms(of, gn) + _silu(gb_ref[:, sl]) * _rms(ob, gn)


def _retention(p3, dec_l, gn_l, *, tq, rope_tabs=None, s0=None, emit_state=False):
    b, l, _ = p3.shape
    rope = rope_tabs is not None
    has_s0 = s0 is not None
    w2 = 2 * DVB
    dec_b = jnp.broadcast_to(dec_l.astype(f32)[:, :, None, None], (2, HB, SUBLANES, LANES))
    in_specs = [
        pl.BlockSpec((None, tq, LANES), lambda i, j, qi: (i, qi, _QB // LANES + j)),
        pl.BlockSpec((None, l, LANES), lambda i, j, qi: (i, 0, _KB // LANES + j)),
        pl.BlockSpec((None, l, w2), lambda i, j, qi: (i, 0, _VB // w2 + j)),
        pl.BlockSpec((None, tq, w2), lambda i, j, qi: (i, qi, _GF // w2 + j)),
        pl.BlockSpec((None, tq, w2), lambda i, j, qi: (i, qi, _GB // w2 + j)),
        pl.BlockSpec((2, 2, SUBLANES, LANES), lambda i, j, qi: (0, j, 0, 0)),
        pl.BlockSpec((1, DVB), lambda i, j, qi: (0, 0)),
    ]
    args = [p3, p3, p3, p3, p3, dec_b, gn_l.reshape(1, DVB)]
    if rope:
        tab_q = pl.BlockSpec((tq, LANES), lambda i, j, qi: (qi, 0))
        tab_k = pl.BlockSpec((l, LANES), lambda i, j, qi: (0, 0))
        in_specs += [tab_q, tab_q, tab_q, tab_k, tab_k, tab_k]
        args += [*rope_tabs, *rope_tabs]
    state_spec = pl.BlockSpec((None, 2, 2, DKB, DVB), lambda i, j, qi: (i, 0, j, 0, 0))
    if has_s0:
        in_specs.append(state_spec)
        args.append(s0)
    out_shape = [jax.ShapeDtypeStruct((b, l, HB * DVB), f32)]
    out_specs = [pl.BlockSpec((None, tq, w2), lambda i, j, qi: (i, qi, j))]
    if emit_state:
        assert tq == l
        out_shape.append(jax.ShapeDtypeStruct((b, 2, HB, DKB, DVB), f32))
        out_specs.append(state_spec)
    return pl.pallas_call(
        functools.partial(_ret_kernel, l=l, tq=tq, rope=rope, has_s0=has_s0, emit_state=emit_state),
        out_shape=out_shape,
        grid=(b, HB // 2, l // tq),
        in_specs=in_specs,
        out_specs=out_specs,
        compiler_params=_params("arbitrary", "arbitrary", "arbitrary"),
        name="retention",
    )(*args)


def _merge_kernel(ya, yb, yc, m0, m1, m2, x_ref, g_ref, wbr, wout, o_ref):
    y = (_sigmoid(m0[...]) * _dot(ya[...].astype(bf16), wbr[0])
         + _sigmoid(m1[...]) * _dot(yb[...].astype(bf16), wbr[1])
         + _sigmoid(m2[...]) * _dot(yc[...].astype(bf16), wbr[2]))
    o_ref[...] = x_ref[...] + g_ref[...] * _dot(y.astype(bf16), wout[...])


def _merge_out(ya, yb, yc, p, x, g1, wbr, wout, rows_per_cond, *, tm):
    t, d = x.shape
    bw = ya.shape[1]
    branch = pl.BlockSpec((tm, bw), lambda i: (i, 0))
    gate = lambda n: pl.BlockSpec((tm, d), lambda i: (i, _MG // d + n))
    return pl.pallas_call(
        _merge_kernel,
        out_shape=jax.ShapeDtypeStruct((t, d), f32),
        grid=(t // tm,),
        in_specs=[
            branch, branch, branch, gate(0), gate(1), gate(2),
            pl.BlockSpec((tm, d), lambda i: (i, 0)),
            pl.BlockSpec((None, 1, d), lambda i: ((i * tm) // rows_per_cond, 0, 0)),
            pl.BlockSpec((3, bw, d), lambda i: (0, 0, 0)),
            pl.BlockSpec((d, d), lambda i: (0, 0)),
        ],
        out_specs=pl.BlockSpec((tm, d), lambda i: (i, 0)),
        compiler_params=_params("arbitrary"),
        name="merge_out",
    )(ya, yb, yc, p, p, p, x, g1, wbr, wout)


def _extract_top(s, lane_f, width, n, emit):
    for k in range(n):
        m = jnp.max(s, axis=-1, keepdims=True)
        idx = jnp.min(jnp.where(s == m, lane_f, float(width)), axis=-1, keepdims=True)
        hit = lane_f == idx
        emit(k, m, idx, hit)
        s = jnp.where(hit, -jnp.inf, s)


def _topk_kernel(q_ref, sk_ref, a_ref, b_ref, g_ref):
    tt = q_ref.shape[0]
    k1 = sk_ref[0].astype(bf16)
    k2 = sk_ref[1].astype(bf16)
    half = PEER_DKEY // 2
    ncand = PEER_TOPK * PEER_TOPK
    lane = lax.broadcasted_iota(jnp.int32, (1, LANES), 1)
    lane_f = lane.astype(f32)
    clane = lax.broadcasted_iota(jnp.int32, (1, ncand), 1)
    clane_f = clane.astype(f32)
    chi = lax.shift_right_logical(clane, int(math.log2(PEER_TOPK)))
    clo = lax.bitwise_and(clane, PEER_TOPK - 1)
    out_a = jnp.zeros((tt, LANES), f32)
    out_b = jnp.zeros((tt, LANES), f32)
    out_g = jnp.zeros((tt, LANES), f32)
    for h in range(PEER_HEADS):
        q1 = q_ref[:, h * PEER_DKEY:h * PEER_DKEY + half].astype(bf16)
        q2 = q_ref[:, h * PEER_DKEY + half:(h + 1) * PEER_DKEY].astype(bf16)
        st = {"cv": jnp.zeros((tt, ncand), f32), "ca": jnp.zeros((tt, ncand), f32),
              "cb": jnp.zeros((tt, ncand), f32)}

        def emit1(k, m, idx, hit):
            st["cv"] = jnp.where(chi == k, m, st["cv"])
            st["ca"] = jnp.where(chi == k, idx, st["ca"])

        def emit2(k, m, idx, hit):
            st["cv"] = jnp.where(clo == k, st["cv"] + m, st["cv"])
            st["cb"] = jnp.where(clo == k, idx, st["cb"])

        _extract_top(_dot_nt(q1, k1), lane_f, PEER_NKEYS, PEER_TOPK, emit1)
        _extract_top(_dot_nt(q2, k2), lane_f, PEER_NKEYS, PEER_TOPK, emit2)
        sel = []

        def emit3(k, m, idx, hit):
            a = jnp.max(jnp.where(hit, st["ca"], -1.0), axis=-1, keepdims=True)
            b = jnp.max(jnp.where(hit, st["cb"], -1.0), axis=-1, keepdims=True)
            sel.append((m, a, b))

        _extract_top(st["cv"], clane_f, ncand, PEER_TOPK, emit3)
        e = [jnp.exp(m - sel[0][0]) for (m, _, _) in sel]
        inv = 1.0 / functools.reduce(lambda x, y: x + y, e)
        for k, (_, a, b) in enumerate(sel):
            pos = lane == h * PEER_TOPK + k
            out_a = jnp.where(pos, a, out_a)
            out_b = jnp.where(pos, b, out_b)
            out_g = jnp.where(pos, e[k] * inv, out_g)
    a_ref[...] = out_a
    b_ref[...] = out_b
    g_ref[...] = out_g


def _peer_topk(q, subkeys, *, tt):
    t, n = q.shape
    row = pl.BlockSpec((tt, LANES), lambda i: (i, 0))
    return pl.pallas_call(
        _topk_kernel,
        out_shape=[jax.ShapeDtypeStruct((t, LANES), f32)] * 3,
        grid=(t // tt,),
        in_specs=[
            pl.BlockSpec((tt, n), lambda i: (i, 0)),
            pl.BlockSpec((2, PEER_NKEYS, PEER_DKEY // 2), lambda i: (0, 0, 0)),
        ],
        out_specs=[row, row, row],
        compiler_params=_params("arbitrary"),
        name="peer_topk",
    )(q, subkeys)


_WGROUP = 16


def _wbuild_kernel(a_ref, b_ref, g_ref, w_ref, scr):
    tw = a_ref.shape[0]
    sub = lax.broadcasted_iota(jnp.int32, (PEER_NKEYS, LANES), 0).astype(f32)

    def group(gi, carry):
        t0 = pl.multiple_of(gi * _WGROUP, _WGROUP)
        for tt in range(_WGROUP):
            arow = a_ref[pl.ds(t0 + tt, 1), :]
            brow = b_ref[pl.ds(t0 + tt, 1), :]
            grow = g_ref[pl.ds(t0 + tt, 1), :]
            at = jnp.where(sub == arow, grow, 0.0).astype(bf16)
            bt = jnp.where(sub == brow, 1.0, 0.0).astype(bf16)
            scr[tt * _WPITCH:tt * _WPITCH + PEER_NKEYS, :] = _dot_nt(at, bt)
        for c in range(PEER_NKEYS):
            lo = scr[pl.ds(c, SUBLANES, stride=_WPITCH), :]
            hi = scr[pl.ds(SUBLANES * _WPITCH + c, SUBLANES, stride=_WPITCH), :]
            w_ref[pl.ds(t0, _WGROUP), c * LANES:(c + 1) * LANES] = (
                jnp.concatenate([lo, hi], axis=0).astype(bf16))
        return carry

    lax.fori_loop(0, tw // _WGROUP, group, 0)


def _peer_weights(a, b, g, *, tw):
    t = a.shape[0]
    row = pl.BlockSpec((tw, LANES), lambda i: (i, 0))
    return pl.pallas_call(
        _wbuild_kernel,
        out_shape=jax.ShapeDtypeStruct((t, PEER_N), bf16),
        grid=(t // tw,),
        in_specs=[row, row, row],
        out_specs=pl.BlockSpec((tw, PEER_N), lambda i: (i, 0)),
        scratch_shapes=[pltpu.VMEM((_WGROUP * _WPITCH, LANES), f32)],
        compiler_params=_params("arbitrary"),
        name="peer_weights",
    )(a, b, g)


def _gelu_tanh(x):
    return 0.5 * x * (1.0 + jnp.tanh(math.sqrt(2.0 / math.pi) * (x + 0.044715 * (x * x * x))))


def _expert_kernel(h_ref, u_ref, v_ref, w_ref, x_ref, g_ref, o_ref, acc):
    c = pl.program_id(1)

    @pl.when(c == 0)
    def _():
        acc[...] = jnp.zeros_like(acc)

    a = _gelu_tanh(_dot_nt(h_ref[...], u_ref[...]))
    acc[...] += _dot((a * w_ref[...].astype(f32)).astype(bf16), v_ref[...])

    @pl.when(c == pl.num_programs(1) - 1)
    def _():
        o_ref[...] = x_ref[...] + g_ref[...] * acc[...]


def _peer_experts(h, u, v, w, x, g2, rows_per_cond, *, tt, ec):
    t, d = x.shape
    return pl.pallas_call(
        _expert_kernel,
        out_shape=jax.ShapeDtypeStruct((t, d), f32),
        grid=(t // tt, PEER_N // ec),
        in_specs=[
            pl.BlockSpec((tt, d), lambda i, c: (i, 0)),
            pl.BlockSpec((ec, d), lambda i, c: (c, 0)),
            pl.BlockSpec((ec, d), lambda i, c: (c, 0)),
            pl.BlockSpec((tt, ec), lambda i, c: (i, c)),
            pl.BlockSpec((tt, d), lambda i, c: (i, 0)),
            pl.BlockSpec((None, 1, d), lambda i, c: ((i * tt) // rows_per_cond, 0, 0)),
        ],
        out_specs=pl.BlockSpec((tt, d), lambda i, c: (i, 0)),
        scratch_shapes=[pltpu.VMEM((tt, d), f32)],
        compiler_params=_params("arbitrary", "arbitrary"),
        name="peer_experts",
    )(h, u, v, w, x, g2)


def _final_norm_kernel(x_ref, g_ref, o_ref):
    o_ref[...] = _rms(x_ref[...], g_ref[...])


def _final_norm(x, g, *, tm):
    t, d = x.shape
    return pl.pallas_call(
        _final_norm_kernel,
        out_shape=jax.ShapeDtypeStruct((t, d), f32),
        grid=(t // tm,),
        in_specs=[pl.BlockSpec((tm, d), lambda i: (i, 0)), pl.BlockSpec((1, d), lambda i: (0, 0))],
        out_specs=pl.BlockSpec((tm, d), lambda i: (i, 0)),
        compiler_params=_params("arbitrary"),
        name="final_norm",
    )(x, g.reshape(1, d))


def _rope_tables(l):
    t = jnp.arange(l)
    rows = (t // GRID_W).astype(f32)
    cols = (t % GRID_W).astype(f32)
    half = DA // 2
    freqs = ROPE_BASE ** (-jnp.arange(0, half, 2, dtype=f32) / half)
    lane = np.arange(LANES)
    use_col = (lane % DA) >= half
    upper = (lane % half) >= half // 2
    fr = freqs[lane % (half // 2)]
    ang = jnp.where(use_col[None, :], cols[:, None], rows[:, None]) * fr[None, :]
    cos, sin = jnp.cos(ang), jnp.sin(ang)
    sa = jnp.where(upper[None, :], sin, 0.0)
    sb = jnp.where(upper[None, :], 0.0, -sin)
    return cos, sa, sb


def _split_mod(m):
    return [m[:, None, i * D_MODEL:(i + 1) * D_MODEL] for i in range(6)]


def _trunk_layer(x, nb, l, mod, cache, wts, lam_init, rope_tabs):
    (norm1_l, w_in_l, lam_l, dgn_l, decay_l, rgn_l, nab_l, wbr_l, wout_l, norm2_l,
     pwq_l, psk_l, pu_l, pv_l) = wts
    t = nb * l
    rows_per_cond = t if mod.shape[0] == 1 else l
    sh1, sc1, g1, sh2, sc2, g2 = _split_mod(mod)
    p = _norm_mod_matmul(x, norm1_l, sc1, sh1, w_in_l, rows_per_cond, tm=512, tn=1024)[0]
    p3 = p.reshape(nb, l, IN_TOTAL)
    if cache is None:
        ya, kd, vd = _diff_attn_ctx(p3, lam_l, dgn_l, lam_init)
        yc, kn, vn = _dense_attn_ctx(p3)
        yb, sr = _retention(p3, decay_l, rgn_l, tq=l, emit_state=True)
        new_ctx = (kd, vd, kn, vn, sr)
    else:
        c_dk, c_dv, c_nk, c_nv, c_st = cache
        ya = _diff_attn_lat(p3, c_dk, c_dv, rope_tabs, lam_l, dgn_l, lam_init, tq=256)
        rows = l // GRID_W
        yc = _na_attn_lat(p3, c_nk, c_nv, _na_bias(nab_l, rows, min(NA_ROWS, rows)))
        yb = _retention(p3, decay_l, rgn_l, tq=256, rope_tabs=rope_tabs, s0=c_st)[0]
        new_ctx = None
    bw = HA * 2 * DA
    x = _merge_out(ya.reshape(t, bw), yb.reshape(t, bw), yc.reshape(t, bw), p, x, g1,
                   wbr_l, wout_l, rows_per_cond, tm=256)
    q, h2 = _norm_mod_matmul(x, norm2_l, sc2, sh2, pwq_l, rows_per_cond, tm=512, tn=1024,
                             emit_h=True)
    a, b, g = _peer_topk(q, psk_l, tt=256)
    w = _peer_weights(a, b, g, tw=128)
    x = _peer_experts(h2, pu_l, pv_l, w, x, g2, rows_per_cond, tt=512, ec=1024)
    return x, new_ctx


def kernel(x_prompt, x_sample, c, cache_diff_k, cache_diff_v, cache_na_k, cache_na_v, state_ret,
           c_ctx, w_mod, b_mod, norm1, w_in, diff_lambda, diff_gn, ret_decay, ret_gn, na_bias,
           w_branch, w_out, norm2, peer_wq, peer_subkeys, peer_u, peer_v, norm_f):
    nbp, lp, d = x_prompt.shape
    nbs, ls, _ = x_sample.shape
    n_cond = SUBLANES
    conds = jnp.concatenate([c_ctx[None, :], c, jnp.zeros((n_cond - 1 - nbs, d), f32)], axis=0)
    mod = _modulation(conds, w_mod, b_mod)
    w_in_b = w_in.astype(bf16)
    wbr_b = w_branch.astype(bf16)
    wout_b = w_out.astype(bf16)
    pwq_b = peer_wq.astype(bf16)
    pu_b = peer_u.astype(bf16)
    pv_b = peer_v.astype(bf16)
    rope_tabs = _rope_tables(ls)

    def weights(li):
        return (norm1[li], w_in_b[li], diff_lambda[li], diff_gn[li], ret_decay[li], ret_gn[li],
                na_bias[li], wbr_b[li], wout_b[li], norm2[li], pwq_b[li], peer_subkeys[li],
                pu_b[li], pv_b[li])

    xp = x_prompt.reshape(nbp * lp, d)
    ctx = []
    for li in range(DEPTH):
        lam_init = 0.8 - 0.6 * math.exp(-0.3 * li)
        xp, new_ctx = _trunk_layer(xp, nbp, lp, mod[li, 0:1], None, weights(li), lam_init, None)
        ctx.append(new_ctx)
    y_prompt = _final_norm(xp, norm_f, tm=512).reshape(nbp, lp, d)

    xs = x_sample.reshape(nbs * ls, d)
    for li in range(DEPTH):
        lam_init = 0.8 - 0.6 * math.exp(-0.3 * li)
        cache = (cache_diff_k[:, li], cache_diff_v[:, li], cache_na_k[:, li], cache_na_v[:, li],
                 state_ret[:, li])
        xs, _ = _trunk_layer(xs, nbs, ls, mod[li, 1:1 + nbs], cache, weights(li), lam_init,
                             rope_tabs)
    y_sample = _final_norm(xs, norm_f, tm=512).reshape(nbs, ls, d)

    stack = lambda i: jnp.stack([ctx[li][i] for li in range(DEPTH)], axis=1)
    return (y_prompt, y_sample, stack(0), stack(1), stack(2), stack(3), stack(4))
```

```python
import functools
import math

import numpy as np
import jax
import jax.numpy as jnp
from jax import lax
from jax.experimental import pallas as pl
from jax.experimental.pallas import tpu as pltpu

D_MODEL = 1024
DEPTH = 2
GRID_W = 64
HA = 4
DA = 64
HB = 4
DKB = 64
DVB = 128
HC = 4
DC = 128
NA_ROWS = 8
NA_COLS = 16
PEER_HEADS = 8
PEER_NKEYS = 128
PEER_DKEY = 256
PEER_TOPK = 16
PEER_N = PEER_NKEYS * PEER_NKEYS
ROPE_BASE = 10000.0
EPS = 1e-6

_QA, _KA, _VA = 0, 512, 1024
_QB, _KB, _VB, _GF, _GB = 1536, 1792, 2048, 2560, 3072
_QC, _KC, _VC, _MG = 3584, 4096, 4608, 5120
IN_TOTAL = 8192

LANES = 128
SUBLANES = 8
MASK_VALUE = -1e30
_WPITCH = 136

f32 = jnp.float32
bf16 = jnp.bfloat16


def _dot(a, b):
    return jnp.dot(a, b, preferred_element_type=f32)


def _dot_nt(a, b):
    return lax.dot_general(a, b, (((1,), (1,)), ((), ())), preferred_element_type=f32)


def _dot_tn(a, b):
    return lax.dot_general(a, b, (((0,), (0,)), ((), ())), preferred_element_type=f32)


def _sigmoid(x):
    return 1.0 / (1.0 + jnp.exp(-x))


def _silu(x):
    return x * _sigmoid(x)


def _rms(x, g):
    return x * lax.rsqrt(jnp.mean(x * x, axis=-1, keepdims=True) + EPS) * g


def _softmax(s):
    e = jnp.exp(s - jnp.max(s, axis=-1, keepdims=True))
    return e * (1.0 / jnp.sum(e, axis=-1, keepdims=True))


def _rope(x, c, sa, sb):
    return x * c + pltpu.roll(x, 16, 1) * sa + pltpu.roll(x, LANES - 16, 1) * sb


def _params(*sem):
    return pltpu.CompilerParams(dimension_semantics=sem)


def _mod_kernel(c_ref, w_ref, b_ref, o_ref):
    a = _silu(c_ref[...]).astype(bf16)
    o_ref[...] = _dot(a, w_ref[...].astype(bf16)) + b_ref[...]


def _modulation(conds, w_mod, b_mod):
    nc = conds.shape[0]
    n = w_mod.shape[-1]
    tn = 1536
    return pl.pallas_call(
        _mod_kernel,
        out_shape=jax.ShapeDtypeStruct((DEPTH, nc, n), f32),
        grid=(DEPTH, n // tn),
        in_specs=[
            pl.BlockSpec((nc, D_MODEL), lambda l, j: (0, 0)),
            pl.BlockSpec((None, D_MODEL, tn), lambda l, j: (l, 0, j)),
            pl.BlockSpec((None, 1, tn), lambda l, j: (l, 0, j)),
        ],
        out_specs=pl.BlockSpec((None, nc, tn), lambda l, j: (l, 0, j)),
        compiler_params=_params("arbitrary", "arbitrary"),
        name="modulation",
    )(conds, w_mod, b_mod.reshape(DEPTH, 1, n))


def _nmm_kernel(x_ref, g_ref, sc_ref, sh_ref, w_ref, o_ref, *rest, emit_h, split_out):
    h_scr = rest[-1]

    @pl.when(pl.program_id(1) == 0)
    def _():
        h = _rms(x_ref[...], g_ref[...]) * (1.0 + sc_ref[...]) + sh_ref[...]
        h_scr[...] = h.astype(bf16)
        if emit_h:
            rest[0][...] = h.astype(bf16)

    res = _dot(h_scr[...], w_ref[...])
    if split_out:
        for c in range(res.shape[1] // LANES):
            o_ref[c] = res[:, c * LANES:(c + 1) * LANES]
    else:
        o_ref[...] = res


def _norm_mod_matmul(x, gain, sc, sh, w, rows_per_cond, *, tm, tn, emit_h=False, split_out=False):
    t, d = x.shape
    n = w.shape[1]
    cond_map = lambda i, j: ((i * tm) // rows_per_cond, 0, 0)
    if split_out:
        out_shape = [jax.ShapeDtypeStruct((n // LANES, t, LANES), f32)]
        out_specs = [pl.BlockSpec((tn // LANES, tm, LANES), lambda i, j: (j, i, 0))]
    else:
        out_shape = [jax.ShapeDtypeStruct((t, n), f32)]
        out_specs = [pl.BlockSpec((tm, tn), lambda i, j: (i, j))]
    if emit_h:
        out_shape.append(jax.ShapeDtypeStruct((t, d), bf16))
        out_specs.append(pl.BlockSpec((tm, d), lambda i, j: (i, 0)))
    return pl.pallas_call(
        functools.partial(_nmm_kernel, emit_h=emit_h, split_out=split_out),
        out_shape=out_shape,
        grid=(t // tm, n // tn),
        in_specs=[
            pl.BlockSpec((tm, d), lambda i, j: (i, 0)),
            pl.BlockSpec((1, d), lambda i, j: (0, 0)),
            pl.BlockSpec((None, 1, d), cond_map),
            pl.BlockSpec((None, 1, d), cond_map),
            pl.BlockSpec((d, tn), lambda i, j: (0, j)),
        ],
        out_specs=out_specs,
        scratch_shapes=[pltpu.VMEM((tm, d), bf16)],
        compiler_params=_params("arbitrary", "arbitrary"),
        name="norm_mod_matmul",
    )(x, gain.reshape(1, d), sc, sh, w)


def _lambda(lam_ref, lam_init):
    l = lam_ref[...]
    a = jnp.sum(l[0:1] * l[1:2], axis=-1, keepdims=True)
    b = jnp.sum(l[2:3] * l[3:4], axis=-1, keepdims=True)
    return jnp.exp(a) - jnp.exp(b) + lam_init


def _diff_core(q, kb, vb, lam_ref, gn_ref, lam_init):
    lane = lax.broadcasted_iota(jnp.int32, (1, LANES), 1)
    lo = lane < DA
    q1 = jnp.where(lo, q, 0.0).astype(bf16)
    q2 = jnp.where(lo, 0.0, q).astype(bf16)
    scale = DA ** -0.5
    p1 = _softmax(_dot_nt(q1, kb) * scale).astype(bf16)
    p2 = _softmax(_dot_nt(q2, kb) * scale).astype(bf16)
    o = _dot(p1, vb) - _lambda(lam_ref, lam_init) * _dot(p2, vb)
    return _rms(o, gn_ref[...]) * (1.0 - lam_init)


def _diff_ctx_kernel(q_ref, k_ref, v_ref, lam_ref, gn_ref, o_ref, ko_ref, vo_ref, *, lam_init):
    k = k_ref[...]
    v = v_ref[...]
    ko_ref[...] = k
    vo_ref[...] = v
    o_ref[...] = _diff_core(q_ref[...], k.astype(bf16), v.astype(bf16), lam_ref, gn_ref, lam_init)


def _diff_attn_ctx(p3, lam_l, gn_l, lam_init):
    b, l, _ = p3.shape
    col = lambda off: (lambda i, h: (i, 0, off // LANES + h))
    head_out = pl.BlockSpec((None, None, l, LANES), lambda i, h: (i, h, 0, 0))
    return pl.pallas_call(
        functools.partial(_diff_ctx_kernel, lam_init=lam_init),
        out_shape=[
            jax.ShapeDtypeStruct((b, l, HA * 2 * DA), f32),
            jax.ShapeDtypeStruct((b, HA, l, 2 * DA), f32),
            jax.ShapeDtypeStruct((b, HA, l, 2 * DA), f32),
        ],
        grid=(b, HA),
        in_specs=[
            pl.BlockSpec((None, l, LANES), col(_QA)),
            pl.BlockSpec((None, l, LANES), col(_KA)),
            pl.BlockSpec((None, l, LANES), col(_VA)),
            pl.BlockSpec((4, DA), lambda i, h: (0, 0)),
            pl.BlockSpec((1, 2 * DA), lambda i, h: (0, 0)),
        ],
        out_specs=[pl.BlockSpec((None, l, LANES), lambda i, h: (i, 0, h)), head_out, head_out],
        compiler_params=_params("arbitrary", "arbitrary"),
        name="diff_attn_ctx",
    )(p3, p3, p3, lam_l, gn_l.reshape(1, 2 * DA))


def _diff_lat_kernel(q_ref, k_ref, v_ref, ck_ref, cv_ref, cq, saq, sbq, ckk, sak, sbk,
                     lam_ref, gn_ref, o_ref, k_scr, v_scr, *, lam_init, past):
    @pl.when(pl.program_id(2) == 0)
    def _():
        k_scr[0:past, :] = ck_ref[...].astype(bf16)
        k_scr[past:, :] = _rope(k_ref[...], ckk[...], sak[...], sbk[...]).astype(bf16)
        v_scr[0:past, :] = cv_ref[...].astype(bf16)
        v_scr[past:, :] = v_ref[...].astype(bf16)

    q = _rope(q_ref[...], cq[...], saq[...], sbq[...])
    o_ref[...] = _diff_core(q, k_scr[...], v_scr[...], lam_ref, gn_ref, lam_init)


def _diff_attn_lat(p3, c_dk, c_dv, rope_tabs, lam_l, gn_l, lam_init, *, tq):
    b, l, _ = p3.shape
    past = c_dk.shape[2]
    col = lambda off: (lambda i, h, qi: (i, 0, off // LANES + h))
    cache = pl.BlockSpec((None, None, past, LANES), lambda i, h, qi: (i, h, 0, 0))
    tab_q = pl.BlockSpec((tq, LANES), lambda i, h, qi: (qi, 0))
    tab_k = pl.BlockSpec((l, LANES), lambda i, h, qi: (0, 0))
    return pl.pallas_call(
        functools.partial(_diff_lat_kernel, lam_init=lam_init, past=past),
        out_shape=jax.ShapeDtypeStruct((b, l, HA * 2 * DA), f32),
        grid=(b, HA, l // tq),
        in_specs=[
            pl.BlockSpec((None, tq, LANES), lambda i, h, qi: (i, qi, _QA // LANES + h)),
            pl.BlockSpec((None, l, LANES), col(_KA)),
            pl.BlockSpec((None, l, LANES), col(_VA)),
            cache, cache, tab_q, tab_q, tab_q, tab_k, tab_k, tab_k,
            pl.BlockSpec((4, DA), lambda i, h, qi: (0, 0)),
            pl.BlockSpec((1, 2 * DA), lambda i, h, qi: (0, 0)),
        ],
        out_specs=pl.BlockSpec((None, tq, LANES), lambda i, h, qi: (i, qi, h)),
        scratch_shapes=[pltpu.VMEM((past + l, LANES), bf16), pltpu.VMEM((past + l, LANES), bf16)],
        compiler_params=_params("arbitrary", "arbitrary", "arbitrary"),
        name="diff_attn_lat",
    )(p3, p3, p3, c_dk, c_dv, *rope_tabs, *rope_tabs, lam_l, gn_l.reshape(1, 2 * DA))


def _dense_attn_kernel(q_ref, k_ref, v_ref, o_ref, ko_ref, vo_ref):
    k = k_ref[...]
    v = v_ref[...]
    ko_ref[...] = k
    vo_ref[...] = v
    s = _dot_nt(q_ref[...].astype(bf16), k.astype(bf16)) * (DC ** -0.5)
    o_ref[...] = _dot(_softmax(s).astype(bf16), v.astype(bf16))


def _dense_attn_ctx(p3):
    b, l, _ = p3.shape
    col = lambda off: (lambda i, h: (i, 0, off // LANES + h))
    head_out = pl.BlockSpec((None, None, l, LANES), lambda i, h: (i, h, 0, 0))
    return pl.pallas_call(
        _dense_attn_kernel,
        out_shape=[
            jax.ShapeDtypeStruct((b, l, HC * DC), f32),
            jax.ShapeDtypeStruct((b, HC, l, DC), f32),
            jax.ShapeDtypeStruct((b, HC, l, DC), f32),
        ],
        grid=(b, HC),
        in_specs=[
            pl.BlockSpec((None, l, LANES), col(_QC)),
            pl.BlockSpec((None, l, LANES), col(_KC)),
            pl.BlockSpec((None, l, LANES), col(_VC)),
        ],
        out_specs=[pl.BlockSpec((None, l, LANES), lambda i, h: (i, 0, h)), head_out, head_out],
        compiler_params=_params("arbitrary", "arbitrary"),
        name="dense_attn_ctx",
    )(p3, p3, p3)


def _na_kernel(q_ref, k_ref, v_ref, ck_ref, cv_ref, bias_ref, o_ref, *, rows, kr):
    scale = DC ** -0.5
    ckb = ck_ref[...].astype(bf16)
    cvb = cv_ref[...].astype(bf16)
    for r in range(rows):
        rs = min(max(r - kr // 2, 0), rows - kr)
        qr = q_ref[r * GRID_W:(r + 1) * GRID_W, :].astype(bf16)
        kl = k_ref[rs * GRID_W:(rs + kr) * GRID_W, :].astype(bf16)
        vl = v_ref[rs * GRID_W:(rs + kr) * GRID_W, :].astype(bf16)
        sl = _dot_nt(qr, kl) * scale + bias_ref[r]
        sc = _dot_nt(qr, ckb) * scale
        m = jnp.maximum(jnp.max(sl, axis=-1, keepdims=True), jnp.max(sc, axis=-1, keepdims=True))
        el = jnp.exp(sl - m)
        ec = jnp.exp(sc - m)
        inv = 1.0 / (jnp.sum(el, axis=-1, keepdims=True) + jnp.sum(ec, axis=-1, keepdims=True))
        pl_ = (el * inv).astype(bf16)
        pc_ = (ec * inv).astype(bf16)
        o_ref[r * GRID_W:(r + 1) * GRID_W, :] = _dot(pl_, vl) + _dot(pc_, cvb)


def _na_bias(table, rows, kr):
    r = np.arange(rows)
    rs = np.clip(r - kr // 2, 0, rows - kr)
    cq = np.arange(GRID_W)
    cs = np.clip(cq - NA_COLS // 2, 0, GRID_W - NA_COLS)
    col_ok = (cq[None, :] >= cs[:, None]) & (cq[None, :] < cs[:, None] + NA_COLS)
    pad = GRID_W - NA_COLS
    ext = jnp.pad(table.astype(f32), ((0, 0), (0, 0), (pad, pad)), mode="edge")
    tz = jnp.stack([ext[:, :, GRID_W - 1 - q:2 * GRID_W - 1 - q] for q in range(GRID_W)], axis=2)
    tz = jnp.where(col_ok[None, None], tz, MASK_VALUE)
    out = []
    for ri in range(rows):
        base = int(rs[ri]) - ri + NA_ROWS - 1
        blk = tz[:, base:base + kr].transpose(0, 2, 1, 3)
        out.append(blk.reshape(table.shape[0], GRID_W, kr * GRID_W))
    return jnp.stack(out, axis=1)


def _na_attn_lat(p3, c_nk, c_nv, bias):
    b, l, _ = p3.shape
    past = c_nk.shape[2]
    rows = l // GRID_W
    kr = min(NA_ROWS, rows)
    col = lambda off: (lambda i, h: (i, 0, off // LANES + h))
    cache = pl.BlockSpec((None, None, past, LANES), lambda i, h: (i, h, 0, 0))
    return pl.pallas_call(
        functools.partial(_na_kernel, rows=rows, kr=kr),
        out_shape=jax.ShapeDtypeStruct((b, l, HC * DC), f32),
        grid=(b, HC),
        in_specs=[
            pl.BlockSpec((None, l, LANES), col(_QC)),
            pl.BlockSpec((None, l, LANES), col(_KC)),
            pl.BlockSpec((None, l, LANES), col(_VC)),
            cache, cache,
            pl.BlockSpec((None, rows, GRID_W, kr * GRID_W), lambda i, h: (h, 0, 0, 0)),
        ],
        out_specs=pl.BlockSpec((None, l, LANES), lambda i, h: (i, 0, h)),
        compiler_params=_params("arbitrary", "arbitrary"),
        name="na_attn_lat",
    )(p3, p3, p3, c_nk, c_nv, bias)


def _log_sigmoid(x):
    return -(jnp.maximum(-x, 0.0) + jnp.log1p(jnp.exp(-jnp.abs(x))))


def _ret_kernel(*refs, l, tq, rope, has_s0, emit_state):
    it = iter(refs)
    q_ref, k_ref, v_ref, gf_ref, gb_ref, dec_ref, gn_ref = (next(it) for _ in range(7))
    if rope:
        cq, saq, sbq, ckk, sak, sbk = (next(it) for _ in range(6))
    if has_s0:
        s0_ref = next(it)
    y_ref = next(it)
    if emit_state:
        s_ref = next(it)

    qi = pl.program_id(2)
    q = q_ref[...]
    k = k_ref[...]
    if rope:
        q = _rope(q, cq[...], saq[...], sbq[...])
        k = _rope(k, ckk[...], sak[...], sbk[...])
    k = k * (DKB ** -0.5)
    kb = k.astype(bf16)
    lane = lax.broadcasted_iota(jnp.int32, (1, LANES), 1)
    tpos = (qi * tq + lax.broadcasted_iota(jnp.int32, (tq, 1), 0)).astype(f32)
    spos_row = lax.broadcasted_iota(jnp.int32, (1, l), 1).astype(f32)
    diff = tpos - spos_row
    lg = _log_sigmoid(dec_ref[...])
    gn = gn_ref[...]
    for hh in range(2):
        mh = (lane >= DKB * hh) & (lane < DKB * (hh + 1))
        qh = jnp.where(mh, q, 0.0)
        qk = _dot_nt(qh.astype(bf16), kb)
        lgf = lg[0, hh][0:1, 0:1]
        lgb = lg[1, hh][0:1, 0:1]
        df = jnp.where(diff >= 0, jnp.exp(lgf * jnp.maximum(diff, 0.0)), 0.0)
        db = jnp.where(diff <= 0, jnp.exp(lgb * jnp.maximum(-diff, 0.0)), 0.0)
        vh = v_ref[:, hh * DVB:(hh + 1) * DVB].astype(bf16)
        of = _dot((qk * df).astype(bf16), vh)
        ob = _dot((qk * db).astype(bf16), vh)
        if has_s0:
            qf = (qh * jnp.exp(lgf * (tpos + 1.0))).astype(bf16)
            qb = (qh * jnp.exp(lgb * (float(l) - tpos))).astype(bf16)
            of = of + _dot(qf, s0_ref[0].reshape(2 * DKB, DVB).astype(bf16))
            ob = ob + _dot(qb, s0_ref[1].reshape(2 * DKB, DVB).astype(bf16))
        if emit_state:
            spos = lax.broadcasted_iota(jnp.int32, (l, 1), 0).astype(f32)
            kf = (k * jnp.exp(lgf * (float(l) - 1.0 - spos))).astype(bf16)
            kw = (k * jnp.exp(lgb * spos)).astype(bf16)
            s_ref[0, hh] = _dot_tn(kf, vh)[hh * DKB:(hh + 1) * DKB, :]
            s_ref[1, hh] = _dot_tn(kw, vh)[hh * DKB:(hh + 1) * DKB, :]
        sl = slice(hh * DVB, (hh + 1) * DVB)
        y_ref[:, sl] = _silu(gf_ref[:, sl]) * _rms(of, gn) + _silu(gb_ref[:, sl]) * _rms(ob, gn)


def _retention(p3, dec_l, gn_l, *, tq, rope_tabs=None, s0=None, emit_state=False):
    b, l, _ = p3.shape
    rope = rope_tabs is not None
    has_s0 = s0 is not None
    w2 = 2 * DVB
    dec_b = jnp.broadcast_to(dec_l.astype(f32)[:, :, None, None], (2, HB, SUBLANES, LANES))
    in_specs = [
        pl.BlockSpec((None, tq, LANES), lambda i, j, qi: (i, qi, _QB // LANES + j)),
        pl.BlockSpec((None, l, LANES), lambda i, j, qi: (i, 0, _KB // LANES + j)),
        pl.BlockSpec((None, l, w2), lambda i, j, qi: (i, 0, _VB // w2 + j)),
        pl.BlockSpec((None, tq, w2), lambda i, j, qi: (i, qi, _GF // w2 + j)),
        pl.BlockSpec((None, tq, w2), lambda i, j, qi: (i, qi, _GB // w2 + j)),
        pl.BlockSpec((2, 2, SUBLANES, LANES), lambda i, j, qi: (0, j, 0, 0)),
        pl.BlockSpec((1, DVB), lambda i, j, qi: (0, 0)),
    ]
    args = [p3, p3, p3, p3, p3, dec_b, gn_l.reshape(1, DVB)]
    if rope:
        tab_q = pl.BlockSpec((tq, LANES), lambda i, j, qi: (qi, 0))
        tab_k = pl.BlockSpec((l, LANES), lambda i, j, qi: (0, 0))
        in_specs += [tab_q, tab_q, tab_q, tab_k, tab_k, tab_k]
        args += [*rope_tabs, *rope_tabs]
    state_spec = pl.BlockSpec((None, 2, 2, DKB, DVB), lambda i, j, qi: (i, 0, j, 0, 0))
    if has_s0:
        in_specs.append(state_spec)
        args.append(s0)
    out_shape = [jax.ShapeDtypeStruct((b, l, HB * DVB), f32)]
    out_specs = [pl.BlockSpec((None, tq, w2), lambda i, j, qi: (i, qi, j))]
    if emit_state:
        assert tq == l
        out_shape.append(jax.ShapeDtypeStruct((b, 2, HB, DKB, DVB), f32))
        out_specs.append(state_spec)
    return pl.pallas_call(
        functools.partial(_ret_kernel, l=l, tq=tq, rope=rope, has_s0=has_s0, emit_state=emit_state),
        out_shape=out_shape,
        grid=(b, HB // 2, l // tq),
        in_specs=in_specs,
        out_specs=out_specs,
        compiler_params=_params("arbitrary", "arbitrary", "arbitrary"),
        name="retention",
    )(*args)


def _merge_kernel(ya, yb, yc, m0, m1, m2, x_ref, g_ref, wbr, wout, o_ref):
    y = (_sigmoid(m0[...]) * _dot(ya[...].astype(bf16), wbr[0])
         + _sigmoid(m1[...]) * _dot(yb[...].astype(bf16), wbr[1])
         + _sigmoid(m2[...]) * _dot(yc[...].astype(bf16), wbr[2]))
    o_ref[...] = x_ref[...] + g_ref[...] * _dot(y.astype(bf16), wout[...])


def _merge_out(ya, yb, yc, p, x, g1, wbr, wout, rows_per_cond, *, tm):
    t, d = x.shape
    bw = ya.shape[1]
    branch = pl.BlockSpec((tm, bw), lambda i: (i, 0))
    gate = lambda n: pl.BlockSpec((tm, d), lambda i: (i, _MG // d + n))
    return pl.pallas_call(
        _merge_kernel,
        out_shape=jax.ShapeDtypeStruct((t, d), f32),
        grid=(t // tm,),
        in_specs=[
            branch, branch, branch, gate(0), gate(1), gate(2),
            pl.BlockSpec((tm, d), lambda i: (i, 0)),
            pl.BlockSpec((None, 1, d), lambda i: ((i * tm) // rows_per_cond, 0, 0)),
            pl.BlockSpec((3, bw, d), lambda i: (0, 0, 0)),
            pl.BlockSpec((d, d), lambda i: (0, 0)),
        ],
        out_specs=pl.BlockSpec((tm, d), lambda i: (i, 0)),
        compiler_params=_params("arbitrary"),
        name="merge_out",
    )(ya, yb, yc, p, p, p, x, g1, wbr, wout)


_PAIRS = [(k, j) for k in range(PEER_TOPK) for j in range(PEER_TOPK)
          if (k + 1) * (j + 1) <= PEER_TOPK]
_PAIR_VREGS = -(-len(_PAIRS) // SUBLANES)


def _allreduce_sublanes(x, op):
    for s in (4, 2, 1):
        x = op(x, pltpu.roll(x, s, 0))
    return x


def _extract_top(s3, pos3, n, payload=None):
    big = float(s3.shape[0] * SUBLANES)
    vals, sel = [], []
    for _ in range(n):
        m = _allreduce_sublanes(jnp.max(s3, axis=0), jnp.maximum)
        cand = jnp.where(s3 == m[None], pos3, big)
        p = _allreduce_sublanes(jnp.min(cand, axis=0), jnp.minimum)
        hit = cand == p[None]
        vals.append(m)
        if payload is None:
            sel.append(p)
        else:
            sel.append(_allreduce_sublanes(jnp.max(jnp.where(hit, payload, -1.0), axis=0), jnp.maximum))
        s3 = jnp.where(hit, -jnp.inf, s3)
    return vals, sel


def _pack_rows(rows, fill):
    sub = lax.broadcasted_iota(jnp.int32, (SUBLANES, LANES), 0)
    out = []
    for r in range(-(-len(rows) // SUBLANES)):
        v = jnp.full((SUBLANES, LANES), fill, f32)
        for s, x in enumerate(rows[r * SUBLANES:(r + 1) * SUBLANES]):
            v = jnp.where(sub == s, x, v)
        out.append(v)
    return jnp.stack(out, axis=0)


def _topk_kernel(q_ref, sk_ref, a_ref, b_ref, g_ref, at_scr, bt_scr, gt_scr):
    tt = q_ref.shape[1]
    k1 = sk_ref[0].astype(bf16)
    k2 = sk_ref[1].astype(bf16)
    nrow = PEER_NKEYS // SUBLANES
    pos_key = (lax.broadcasted_iota(jnp.int32, (nrow, SUBLANES, LANES), 0) * SUBLANES
               + lax.broadcasted_iota(jnp.int32, (nrow, SUBLANES, LANES), 1)).astype(f32)
    pos_pair = (lax.broadcasted_iota(jnp.int32, (_PAIR_VREGS, SUBLANES, LANES), 0) * SUBLANES
                + lax.broadcasted_iota(jnp.int32, (_PAIR_VREGS, SUBLANES, LANES), 1)).astype(f32)

    def head(h, carry):
        for grp in range(tt // LANES):
            tok = slice(grp * LANES, (grp + 1) * LANES)
            q1 = q_ref[2 * h, tok, :].astype(bf16)
            q2 = q_ref[2 * h + 1, tok, :].astype(bf16)
            s1 = _dot_nt(k1, q1).reshape(nrow, SUBLANES, LANES)
            s2 = _dot_nt(k2, q2).reshape(nrow, SUBLANES, LANES)
            v1, i1 = _extract_top(s1, pos_key, PEER_TOPK)
            v2, i2 = _extract_top(s2, pos_key, PEER_TOPK)
            cand = _pack_rows([v1[k] + v2[j] for k, j in _PAIRS], -jnp.inf)
            cid = _pack_rows([i1[k] * float(PEER_NKEYS) + i2[j] for k, j in _PAIRS], -1.0)
            sc, ids = _extract_top(cand, pos_pair, PEER_TOPK, payload=cid)
            e = [jnp.exp(m - sc[0]) for m in sc]
            inv = 1.0 / functools.reduce(lambda x, y: x + y, e)
            a = [jnp.floor(i * (1.0 / PEER_NKEYS)) for i in ids]
            b = [i - x * float(PEER_NKEYS) for i, x in zip(ids, a)]
            rows = pl.ds(pl.multiple_of(h * PEER_TOPK, PEER_TOPK), PEER_TOPK)
            at_scr[rows, tok] = _pack_rows(a, 0.0).reshape(PEER_TOPK, LANES)
            bt_scr[rows, tok] = _pack_rows(b, 0.0).reshape(PEER_TOPK, LANES)
            gt_scr[rows, tok] = _pack_rows([x * inv for x in e], 0.0).reshape(PEER_TOPK, LANES)
        return carry

    lax.fori_loop(0, PEER_HEADS, head, 0)
    for grp in range(tt // LANES):
        tok = slice(grp * LANES, (grp + 1) * LANES)
        a_ref[tok, :] = at_scr[:, tok].T
        b_ref[tok, :] = bt_scr[:, tok].T
        g_ref[tok, :] = gt_scr[:, tok].T


def _peer_topk(q, subkeys, *, tt):
    nq, t, _ = q.shape
    row = pl.BlockSpec((tt, LANES), lambda i: (i, 0))
    return pl.pallas_call(
        _topk_kernel,
        out_shape=[jax.ShapeDtypeStruct((t, LANES), f32)] * 3,
        grid=(t // tt,),
        in_specs=[
            pl.BlockSpec((nq, tt, LANES), lambda i: (0, i, 0)),
            pl.BlockSpec((2, PEER_NKEYS, PEER_DKEY // 2), lambda i: (0, 0, 0)),
        ],
        out_specs=[row, row, row],
        scratch_shapes=[pltpu.VMEM((PEER_HEADS * PEER_TOPK, tt), f32)] * 3,
        compiler_params=_params("arbitrary"),
        name="peer_topk",
    )(q, subkeys)


_WGROUP = 16


def _wbuild_kernel(a_ref, b_ref, g_ref, w_ref, scr):
    tw = a_ref.shape[0]
    sub = lax.broadcasted_iota(jnp.int32, (PEER_NKEYS, LANES), 0).astype(f32)

    def group(gi, carry):
        t0 = pl.multiple_of(gi * _WGROUP, _WGROUP)
        for tt in range(_WGROUP):
            arow = a_ref[pl.ds(t0 + tt, 1), :]
            brow = b_ref[pl.ds(t0 + tt, 1), :]
            grow = g_ref[pl.ds(t0 + tt, 1), :]
            at = jnp.where(sub == arow, grow, 0.0).astype(bf16)
            bt = jnp.where(sub == brow, 1.0, 0.0).astype(bf16)
            scr[tt * _WPITCH:tt * _WPITCH + PEER_NKEYS, :] = _dot_nt(at, bt)
        for c in range(PEER_NKEYS):
            lo = scr[pl.ds(c, SUBLANES, stride=_WPITCH), :]
            hi = scr[pl.ds(SUBLANES * _WPITCH + c, SUBLANES, stride=_WPITCH), :]
            w_ref[pl.ds(t0, _WGROUP), c * LANES:(c + 1) * LANES] = (
                jnp.concatenate([lo, hi], axis=0).astype(bf16))
        return carry

    lax.fori_loop(0, tw // _WGROUP, group, 0)


def _peer_weights(a, b, g, *, tw):
    t = a.shape[0]
    row = pl.BlockSpec((tw, LANES), lambda i: (i, 0))
    return pl.pallas_call(
        _wbuild_kernel,
        out_shape=jax.ShapeDtypeStruct((t, PEER_N), bf16),
        grid=(t // tw,),
        in_specs=[row, row, row],
        out_specs=pl.BlockSpec((tw, PEER_N), lambda i: (i, 0)),
        scratch_shapes=[pltpu.VMEM((_WGROUP * _WPITCH, LANES), f32)],
        compiler_params=_params("arbitrary"),
        name="peer_weights",
    )(a, b, g)


def _gelu_tanh(x):
    return 0.5 * x * (1.0 + jnp.tanh(math.sqrt(2.0 / math.pi) * (x + 0.044715 * (x * x * x))))


def _expert_kernel(h_ref, u_ref, v_ref, w_ref, x_ref, g_ref, o_ref, acc):
    c = pl.program_id(1)

    @pl.when(c == 0)
    def _():
        acc[...] = jnp.zeros_like(acc)

    a = _gelu_tanh(_dot_nt(h_ref[...], u_ref[...]))
    acc[...] += _dot((a * w_ref[...].astype(f32)).astype(bf16), v_ref[...])

    @pl.when(c == pl.num_programs(1) - 1)
    def _():
        o_ref[...] = x_ref[...] + g_ref[...] * acc[...]


def _peer_experts(h, u, v, w, x, g2, rows_per_cond, *, tt, ec):
    t, d = x.shape
    return pl.pallas_call(
        _expert_kernel,
        out_shape=jax.ShapeDtypeStruct((t, d), f32),
        grid=(t // tt, PEER_N // ec),
        in_specs=[
            pl.BlockSpec((tt, d), lambda i, c: (i, 0)),
            pl.BlockSpec((ec, d), lambda i, c: (c, 0)),
            pl.BlockSpec((ec, d), lambda i, c: (c, 0)),
            pl.BlockSpec((tt, ec), lambda i, c: (i, c)),
            pl.BlockSpec((tt, d), lambda i, c: (i, 0)),
            pl.BlockSpec((None, 1, d), lambda i, c: ((i * tt) // rows_per_cond, 0, 0)),
        ],
        out_specs=pl.BlockSpec((tt, d), lambda i, c: (i, 0)),
        scratch_shapes=[pltpu.VMEM((tt, d), f32)],
        compiler_params=_params("arbitrary", "arbitrary"),
        name="peer_experts",
    )(h, u, v, w, x, g2)


def _final_norm_kernel(x_ref, g_ref, o_ref):
    o_ref[...] = _rms(x_ref[...], g_ref[...])


def _final_norm(x, g, *, tm):
    t, d = x.shape
    return pl.pallas_call(
        _final_norm_kernel,
        out_shape=jax.ShapeDtypeStruct((t, d), f32),
        grid=(t // tm,),
        in_specs=[pl.BlockSpec((tm, d), lambda i: (i, 0)), pl.BlockSpec((1, d), lambda i: (0, 0))],
        out_specs=pl.BlockSpec((tm, d), lambda i: (i, 0)),
        compiler_params=_params("arbitrary"),
        name="final_norm",
    )(x, g.reshape(1, d))


def _rope_tables(l):
    t = jnp.arange(l)
    rows = (t // GRID_W).astype(f32)
    cols = (t % GRID_W).astype(f32)
    half = DA // 2
    freqs = ROPE_BASE ** (-jnp.arange(0, half, 2, dtype=f32) / half)
    lane = np.arange(LANES)
    use_col = (lane % DA) >= half
    upper = (lane % half) >= half // 2
    fr = freqs[lane % (half // 2)]
    ang = jnp.where(use_col[None, :], cols[:, None], rows[:, None]) * fr[None, :]
    cos, sin = jnp.cos(ang), jnp.sin(ang)
    sa = jnp.where(upper[None, :], sin, 0.0)
    sb = jnp.where(upper[None, :], 0.0, -sin)
    return cos, sa, sb


def _split_mod(m):
    return [m[:, None, i * D_MODEL:(i + 1) * D_MODEL] for i in range(6)]


def _trunk_layer(x, nb, l, mod, cache, wts, lam_init, rope_tabs):
    (norm1_l, w_in_l, lam_l, dgn_l, decay_l, rgn_l, nab_l, wbr_l, wout_l, norm2_l,
     pwq_l, psk_l, pu_l, pv_l) = wts
    t = nb * l
    rows_per_cond = t if mod.shape[0] == 1 else l
    sh1, sc1, g1, sh2, sc2, g2 = _split_mod(mod)
    p = _norm_mod_matmul(x, norm1_l, sc1, sh1, w_in_l, rows_per_cond, tm=512, tn=1024)[0]
    p3 = p.reshape(nb, l, IN_TOTAL)
    if cache is None:
        ya, kd, vd = _diff_attn_ctx(p3, lam_l, dgn_l, lam_init)
        yc, kn, vn = _dense_attn_ctx(p3)
        yb, sr = _retention(p3, decay_l, rgn_l, tq=l, emit_state=True)
        new_ctx = (kd, vd, kn, vn, sr)
    else:
        c_dk, c_dv, c_nk, c_nv, c_st = cache
        ya = _diff_attn_lat(p3, c_dk, c_dv, rope_tabs, lam_l, dgn_l, lam_init, tq=256)
        rows = l // GRID_W
        yc = _na_attn_lat(p3, c_nk, c_nv, _na_bias(nab_l, rows, min(NA_ROWS, rows)))
        yb = _retention(p3, decay_l, rgn_l, tq=256, rope_tabs=rope_tabs, s0=c_st)[0]
        new_ctx = None
    bw = HA * 2 * DA
    x = _merge_out(ya.reshape(t, bw), yb.reshape(t, bw), yc.reshape(t, bw), p, x, g1,
                   wbr_l, wout_l, rows_per_cond, tm=256)
    q, h2 = _norm_mod_matmul(x, norm2_l, sc2, sh2, pwq_l, rows_per_cond, tm=512, tn=1024,
                             emit_h=True, split_out=True)
    a, b, g = _peer_topk(q, psk_l, tt=256)
    w = _peer_weights(a, b, g, tw=128)
    x = _peer_experts(h2, pu_l, pv_l, w, x, g2, rows_per_cond, tt=512, ec=1024)
    return x, new_ctx


def kernel(x_prompt, x_sample, c, cache_diff_k, cache_diff_v, cache_na_k, cache_na_v, state_ret,
           c_ctx, w_mod, b_mod, norm1, w_in, diff_lambda, diff_gn, ret_decay, ret_gn, na_bias,
           w_branch, w_out, norm2, peer_wq, peer_subkeys, peer_u, peer_v, norm_f):
    nbp, lp, d = x_prompt.shape
    nbs, ls, _ = x_sample.shape
    n_cond = SUBLANES
    conds = jnp.concatenate([c_ctx[None, :], c, jnp.zeros((n_cond - 1 - nbs, d), f32)], axis=0)
    mod = _modulation(conds, w_mod, b_mod)
    w_in_b = w_in.astype(bf16)
    wbr_b = w_branch.astype(bf16)
    wout_b = w_out.astype(bf16)
    pwq_b = peer_wq.astype(bf16)
    pu_b = peer_u.astype(bf16)
    pv_b = peer_v.astype(bf16)
    rope_tabs = _rope_tables(ls)

    def weights(li):
        return (norm1[li], w_in_b[li], diff_lambda[li], diff_gn[li], ret_decay[li], ret_gn[li],
                na_bias[li], wbr_b[li], wout_b[li], norm2[li], pwq_b[li], peer_subkeys[li],
                pu_b[li], pv_b[li])

    xp = x_prompt.reshape(nbp * lp, d)
    ctx = []
    for li in range(DEPTH):
        lam_init = 0.8 - 0.6 * math.exp(-0.3 * li)
        xp, new_ctx = _trunk_layer(xp, nbp, lp, mod[li, 0:1], None, weights(li), lam_init, None)
        ctx.append(new_ctx)
    y_prompt = _final_norm(xp, norm_f, tm=512).reshape(nbp, lp, d)

    xs = x_sample.reshape(nbs * ls, d)
    for li in range(DEPTH):
        lam_init = 0.8 - 0.6 * math.exp(-0.3 * li)
        cache = (cache_diff_k[:, li], cache_diff_v[:, li], cache_na_k[:, li], cache_na_v[:, li],
                 state_ret[:, li])
        xs, _ = _trunk_layer(xs, nbs, ls, mod[li, 1:1 + nbs], cache, weights(li), lam_init,
                             rope_tabs)
    y_sample = _final_norm(xs, norm_f, tm=512).reshape(nbs, ls, d)

    stack = lambda i: jnp.stack([ctx[li][i] for li in range(DEPTH)], axis=1)
    return (y_prompt, y_sample, stack(0), stack(1), stack(2), stack(3), stack(4))
```

```python
import functools
import math

import numpy as np
import jax
import jax.numpy as jnp
from jax import lax
from jax.experimental import pallas as pl
from jax.experimental.pallas import tpu as pltpu

D_MODEL = 1024
DEPTH = 2
GRID_W = 64
HA = 4
DA = 64
HB = 4
DKB = 64
DVB = 128
HC = 4
DC = 128
NA_ROWS = 8
NA_COLS = 16
PEER_HEADS = 8
PEER_NKEYS = 128
PEER_DKEY = 256
PEER_TOPK = 16
PEER_N = PEER_NKEYS * PEER_NKEYS
ROPE_BASE = 10000.0
EPS = 1e-6

_QA, _KA, _VA = 0, 512, 1024
_QB, _KB, _VB, _GF, _GB = 1536, 1792, 2048, 2560, 3072
_QC, _KC, _VC, _MG = 3584, 4096, 4608, 5120
IN_TOTAL = 8192

LANES = 128
SUBLANES = 8
MASK_VALUE = -1e30
_WPITCH = 136

f32 = jnp.float32
bf16 = jnp.bfloat16


def _dot(a, b):
    return jnp.dot(a, b, preferred_element_type=f32)


def _dot_nt(a, b):
    return lax.dot_general(a, b, (((1,), (1,)), ((), ())), preferred_element_type=f32)


def _dot_tn(a, b):
    return lax.dot_general(a, b, (((0,), (0,)), ((), ())), preferred_element_type=f32)


def _sigmoid(x):
    return 1.0 / (1.0 + jnp.exp(-x))


def _silu(x):
    return x * _sigmoid(x)


def _rms(x, g):
    return x * lax.rsqrt(jnp.mean(x * x, axis=-1, keepdims=True) + EPS) * g


def _softmax(s):
    e = jnp.exp(s - jnp.max(s, axis=-1, keepdims=True))
    return e * (1.0 / jnp.sum(e, axis=-1, keepdims=True))


def _rope(x, c, sa, sb):
    return x * c + pltpu.roll(x, 16, 1) * sa + pltpu.roll(x, LANES - 16, 1) * sb


def _params(*sem):
    return pltpu.CompilerParams(dimension_semantics=sem)


def _mod_kernel(c_ref, w_ref, b_ref, o_ref):
    a = _silu(c_ref[...]).astype(bf16)
    o_ref[...] = _dot(a, w_ref[...].astype(bf16)) + b_ref[...]


def _modulation(conds, w_mod, b_mod):
    nc = conds.shape[0]
    n = w_mod.shape[-1]
    tn = 1536
    return pl.pallas_call(
        _mod_kernel,
        out_shape=jax.ShapeDtypeStruct((DEPTH, nc, n), f32),
        grid=(DEPTH, n // tn),
        in_specs=[
            pl.BlockSpec((nc, D_MODEL), lambda l, j: (0, 0)),
            pl.BlockSpec((None, D_MODEL, tn), lambda l, j: (l, 0, j)),
            pl.BlockSpec((None, 1, tn), lambda l, j: (l, 0, j)),
        ],
        out_specs=pl.BlockSpec((None, nc, tn), lambda l, j: (l, 0, j)),
        compiler_params=_params("arbitrary", "arbitrary"),
        name="modulation",
    )(conds, w_mod, b_mod.reshape(DEPTH, 1, n))


def _nmm_kernel(x_ref, g_ref, sc_ref, sh_ref, w_ref, o_ref, *rest, emit_h, split_out):
    h_scr = rest[-1]
    j = pl.program_id(1)

    @pl.when(j == 0)
    def _():
        h = _rms(x_ref[...], g_ref[...]) * (1.0 + sc_ref[...]) + sh_ref[...]
        h_scr[...] = h.astype(bf16)
        if emit_h:
            rest[0][...] = h.astype(bf16)

    res = _dot(h_scr[...], w_ref[j])
    if split_out:
        for c in range(res.shape[1] // LANES):
            o_ref[c] = res[:, c * LANES:(c + 1) * LANES]
    else:
        o_ref[...] = res


def _column_slabs(w, tn):
    d, n = w.shape
    return w.reshape(d, n // tn, tn).transpose(1, 0, 2).astype(bf16)


def _norm_mod_matmul(x, gain, sc, sh, w_slabs, rows_per_cond, *, tm, emit_h=False, split_out=False):
    t, d = x.shape
    nslab, _, tn = w_slabs.shape
    n = nslab * tn
    cond_map = lambda i, j: ((i * tm) // rows_per_cond, 0, 0)
    if split_out:
        out_shape = [jax.ShapeDtypeStruct((n // LANES, t, LANES), f32)]
        out_specs = [pl.BlockSpec((tn // LANES, tm, LANES), lambda i, j: (j, i, 0))]
    else:
        out_shape = [jax.ShapeDtypeStruct((t, n), f32)]
        out_specs = [pl.BlockSpec((tm, tn), lambda i, j: (i, j))]
    if emit_h:
        out_shape.append(jax.ShapeDtypeStruct((t, d), bf16))
        out_specs.append(pl.BlockSpec((tm, d), lambda i, j: (i, 0)))
    return pl.pallas_call(
        functools.partial(_nmm_kernel, emit_h=emit_h, split_out=split_out),
        out_shape=out_shape,
        grid=(t // tm, nslab),
        in_specs=[
            pl.BlockSpec((tm, d), lambda i, j: (i, 0)),
            pl.BlockSpec((1, d), lambda i, j: (0, 0)),
            pl.BlockSpec((None, 1, d), cond_map),
            pl.BlockSpec((None, 1, d), cond_map),
            pl.BlockSpec((nslab, d, tn), lambda i, j: (0, 0, 0), pipeline_mode=pl.Buffered(1)),
        ],
        out_specs=out_specs,
        scratch_shapes=[pltpu.VMEM((tm, d), bf16)],
        compiler_params=_params("arbitrary", "arbitrary"),
        name="norm_mod_matmul",
    )(x, gain.reshape(1, d), sc, sh, w_slabs)


def _lambda(lam_ref, lam_init):
    l = lam_ref[...]
    a = jnp.sum(l[0:1] * l[1:2], axis=-1, keepdims=True)
    b = jnp.sum(l[2:3] * l[3:4], axis=-1, keepdims=True)
    return jnp.exp(a) - jnp.exp(b) + lam_init


def _diff_core(q, kb, vb, lam_ref, gn_ref, lam_init):
    lane = lax.broadcasted_iota(jnp.int32, (1, LANES), 1)
    lo = lane < DA
    q1 = jnp.where(lo, q, 0.0).astype(bf16)
    q2 = jnp.where(lo, 0.0, q).astype(bf16)
    scale = DA ** -0.5
    p1 = _softmax(_dot_nt(q1, kb) * scale).astype(bf16)
    p2 = _softmax(_dot_nt(q2, kb) * scale).astype(bf16)
    o = _dot(p1, vb) - _lambda(lam_ref, lam_init) * _dot(p2, vb)
    return _rms(o, gn_ref[...]) * (1.0 - lam_init)


def _emit_cache(out_ref, prev_ref, cur, h):
    if prev_ref is None:
        out_ref[h] = cur
    else:
        out_ref[0, h] = prev_ref[h]
        out_ref[1, h] = cur


def _diff_ctx_kernel(*refs, lam_init, stacked):
    if stacked:
        q_ref, k_ref, v_ref, kp_ref, vp_ref, lam_ref, gn_ref, o_ref, ko_ref, vo_ref = refs
    else:
        q_ref, k_ref, v_ref, lam_ref, gn_ref, o_ref, ko_ref, vo_ref = refs
        kp_ref = vp_ref = None
    for h in range(HA):
        sl = slice(h * LANES, (h + 1) * LANES)
        k = k_ref[:, sl]
        v = v_ref[:, sl]
        _emit_cache(ko_ref, kp_ref, k, h)
        _emit_cache(vo_ref, vp_ref, v, h)
        o_ref[:, sl] = _diff_core(q_ref[:, sl], k.astype(bf16), v.astype(bf16), lam_ref, gn_ref,
                                  lam_init)


def _head_cache_specs(b, nh, l, dh, prev):
    one = pl.BlockSpec((None, nh, l, dh), lambda i: (i, 0, 0, 0))
    if prev is None:
        return [], jax.ShapeDtypeStruct((b, nh, l, dh), f32), one
    assert DEPTH == 2
    return ([one], jax.ShapeDtypeStruct((b, DEPTH, nh, l, dh), f32),
            pl.BlockSpec((None, DEPTH, nh, l, dh), lambda i: (i, 0, 0, 0, 0)))


def _diff_attn_ctx(p3, lam_l, gn_l, lam_init, prev=None):
    b, l, _ = p3.shape
    w = HA * 2 * DA
    col = lambda off: (lambda i: (i, 0, off // w))
    prev_specs, cache_shape, cache_spec = _head_cache_specs(b, HA, l, 2 * DA, prev)
    return pl.pallas_call(
        functools.partial(_diff_ctx_kernel, lam_init=lam_init, stacked=prev is not None),
        out_shape=[jax.ShapeDtypeStruct((b, l, w), f32), cache_shape, cache_shape],
        grid=(b,),
        in_specs=[
            pl.BlockSpec((None, l, w), col(_QA)),
            pl.BlockSpec((None, l, w), col(_KA)),
            pl.BlockSpec((None, l, w), col(_VA)),
            *prev_specs, *prev_specs,
            pl.BlockSpec((4, DA), lambda i: (0, 0)),
            pl.BlockSpec((1, 2 * DA), lambda i: (0, 0)),
        ],
        out_specs=[pl.BlockSpec((None, l, w), lambda i: (i, 0, 0)), cache_spec, cache_spec],
        compiler_params=_params("arbitrary"),
        name="diff_attn_ctx",
    )(p3, p3, p3, *(prev or ()), lam_l, gn_l.reshape(1, 2 * DA))


def _diff_lat_kernel(q_ref, k_ref, v_ref, ck_ref, cv_ref, cq, saq, sbq, ckk, sak, sbk,
                     lam_ref, gn_ref, o_ref, k_scr, v_scr, *, lam_init, past):
    @pl.when(pl.program_id(2) == 0)
    def _():
        k_scr[0:past, :] = ck_ref[...].astype(bf16)
        k_scr[past:, :] = _rope(k_ref[...], ckk[...], sak[...], sbk[...]).astype(bf16)
        v_scr[0:past, :] = cv_ref[...].astype(bf16)
        v_scr[past:, :] = v_ref[...].astype(bf16)

    q = _rope(q_ref[...], cq[...], saq[...], sbq[...])
    o_ref[...] = _diff_core(q, k_scr[...], v_scr[...], lam_ref, gn_ref, lam_init)


def _diff_attn_lat(p3, c_dk, c_dv, rope_tabs, lam_l, gn_l, lam_init, *, tq):
    b, l, _ = p3.shape
    past = c_dk.shape[2]
    col = lambda off: (lambda i, h, qi: (i, 0, off // LANES + h))
    cache = pl.BlockSpec((None, None, past, LANES), lambda i, h, qi: (i, h, 0, 0))
    tab_q = pl.BlockSpec((tq, LANES), lambda i, h, qi: (qi, 0))
    tab_k = pl.BlockSpec((l, LANES), lambda i, h, qi: (0, 0))
    return pl.pallas_call(
        functools.partial(_diff_lat_kernel, lam_init=lam_init, past=past),
        out_shape=jax.ShapeDtypeStruct((b, l, HA * 2 * DA), f32),
        grid=(b, HA, l // tq),
        in_specs=[
            pl.BlockSpec((None, tq, LANES), lambda i, h, qi: (i, qi, _QA // LANES + h)),
            pl.BlockSpec((None, l, LANES), col(_KA)),
            pl.BlockSpec((None, l, LANES), col(_VA)),
            cache, cache, tab_q, tab_q, tab_q, tab_k, tab_k, tab_k,
            pl.BlockSpec((4, DA), lambda i, h, qi: (0, 0)),
            pl.BlockSpec((1, 2 * DA), lambda i, h, qi: (0, 0)),
        ],
        out_specs=pl.BlockSpec((None, tq, LANES), lambda i, h, qi: (i, qi, h)),
        scratch_shapes=[pltpu.VMEM((past + l, LANES), bf16), pltpu.VMEM((past + l, LANES), bf16)],
        compiler_params=_params("arbitrary", "arbitrary", "arbitrary"),
        name="diff_attn_lat",
    )(p3, p3, p3, c_dk, c_dv, *rope_tabs, *rope_tabs, lam_l, gn_l.reshape(1, 2 * DA))


def _dense_attn_kernel(*refs, stacked):
    if stacked:
        q_ref, k_ref, v_ref, kp_ref, vp_ref, o_ref, ko_ref, vo_ref = refs
    else:
        q_ref, k_ref, v_ref, o_ref, ko_ref, vo_ref = refs
        kp_ref = vp_ref = None
    for h in range(HC):
        sl = slice(h * DC, (h + 1) * DC)
        k = k_ref[:, sl]
        v = v_ref[:, sl]
        _emit_cache(ko_ref, kp_ref, k, h)
        _emit_cache(vo_ref, vp_ref, v, h)
        s = _dot_nt(q_ref[:, sl].astype(bf16), k.astype(bf16)) * (DC ** -0.5)
        o_ref[:, sl] = _dot(_softmax(s).astype(bf16), v.astype(bf16))


def _dense_attn_ctx(p3, prev=None):
    b, l, _ = p3.shape
    w = HC * DC
    col = lambda off: (lambda i: (i, 0, off // w))
    prev_specs, cache_shape, cache_spec = _head_cache_specs(b, HC, l, DC, prev)
    return pl.pallas_call(
        functools.partial(_dense_attn_kernel, stacked=prev is not None),
        out_shape=[jax.ShapeDtypeStruct((b, l, w), f32), cache_shape, cache_shape],
        grid=(b,),
        in_specs=[
            pl.BlockSpec((None, l, w), col(_QC)),
            pl.BlockSpec((None, l, w), col(_KC)),
            pl.BlockSpec((None, l, w), col(_VC)),
            *prev_specs, *prev_specs,
        ],
        out_specs=[pl.BlockSpec((None, l, w), lambda i: (i, 0, 0)), cache_spec, cache_spec],
        compiler_params=_params("arbitrary"),
        name="dense_attn_ctx",
    )(p3, p3, p3, *(prev or ()))


def _na_kernel(q_ref, k_ref, v_ref, ck_ref, cv_ref, e_ref, o_ref, tl_scr, tr_scr, *, rows, kr):
    scale = DC ** -0.5
    ckb = ck_ref[...].astype(bf16)
    cvb = cv_ref[...].astype(bf16)
    qcol = lax.broadcasted_iota(jnp.int32, (GRID_W, LANES), 0)
    lane = lax.broadcasted_iota(jnp.int32, (GRID_W, LANES), 1)
    kcol = lax.bitwise_and(lane, GRID_W - 1)
    cs = jnp.clip(qcol - NA_COLS // 2, 0, GRID_W - NA_COLS)
    col_ok = (kcol >= cs) & (kcol < cs + NA_COLS)
    for ro in range(2 * NA_ROWS - 1):
        t = pltpu.roll(jnp.broadcast_to(e_ref[ro:ro + 1, :], (GRID_W, LANES)), 0, 1,
                       stride=1, stride_axis=0)
        tl_scr[ro] = t
        tr_scr[ro] = pltpu.roll(t, GRID_W, 1)
    for r in range(rows):
        rs = min(max(r - kr // 2, 0), rows - kr)
        base = rs - r + NA_ROWS - 1
        bias = jnp.concatenate(
            [jnp.where(col_ok, jnp.where(lane < GRID_W, tl_scr[base + 2 * m], tr_scr[base + 2 * m + 1]),
                       MASK_VALUE) for m in range(kr // 2)], axis=1)
        qr = q_ref[r * GRID_W:(r + 1) * GRID_W, :].astype(bf16)
        kl = k_ref[rs * GRID_W:(rs + kr) * GRID_W, :].astype(bf16)
        vl = v_ref[rs * GRID_W:(rs + kr) * GRID_W, :].astype(bf16)
        sl = _dot_nt(qr, kl) * scale + bias
        sc = _dot_nt(qr, ckb) * scale
        m = jnp.maximum(jnp.max(sl, axis=-1, keepdims=True), jnp.max(sc, axis=-1, keepdims=True))
        el = jnp.exp(sl - m)
        ec = jnp.exp(sc - m)
        inv = 1.0 / (jnp.sum(el, axis=-1, keepdims=True) + jnp.sum(ec, axis=-1, keepdims=True))
        pl_ = (el * inv).astype(bf16)
        pc_ = (ec * inv).astype(bf16)
        o_ref[r * GRID_W:(r + 1) * GRID_W, :] = _dot(pl_, vl) + _dot(pc_, cvb)


def _na_bias_rows(table):
    nc = 2 * NA_COLS - 1
    first, last = table[..., :1], table[..., nc - 1:]
    return jnp.concatenate(
        [table[..., NA_COLS - 1:],
         jnp.broadcast_to(last, (*table.shape[:-1], GRID_W + 1 - NA_COLS)),
         jnp.broadcast_to(first, (*table.shape[:-1], GRID_W - NA_COLS)),
         table[..., :NA_COLS - 1]], axis=-1).astype(f32)


def _na_attn_lat(p3, c_nk, c_nv, table):
    b, l, _ = p3.shape
    past = c_nk.shape[2]
    rows = l // GRID_W
    kr = min(NA_ROWS, rows)
    assert kr % 2 == 0 and 2 * GRID_W == LANES
    nro = 2 * NA_ROWS - 1
    col = lambda off: (lambda i, h: (i, 0, off // LANES + h))
    cache = pl.BlockSpec((None, None, past, LANES), lambda i, h: (i, h, 0, 0))
    return pl.pallas_call(
        functools.partial(_na_kernel, rows=rows, kr=kr),
        out_shape=jax.ShapeDtypeStruct((b, l, HC * DC), f32),
        grid=(b, HC),
        in_specs=[
            pl.BlockSpec((None, l, LANES), col(_QC)),
            pl.BlockSpec((None, l, LANES), col(_KC)),
            pl.BlockSpec((None, l, LANES), col(_VC)),
            cache, cache,
            pl.BlockSpec((None, nro, LANES), lambda i, h: (h, 0, 0)),
        ],
        out_specs=pl.BlockSpec((None, l, LANES), lambda i, h: (i, 0, h)),
        scratch_shapes=[pltpu.VMEM((nro, GRID_W, LANES), f32)] * 2,
        compiler_params=_params("arbitrary", "arbitrary"),
        name="na_attn_lat",
    )(p3, p3, p3, c_nk, c_nv, _na_bias_rows(table))


def _log_sigmoid(x):
    return -(jnp.maximum(-x, 0.0) + jnp.log1p(jnp.exp(-jnp.abs(x))))


def _ret_kernel(*refs, l, tq, rope, has_s0, emit_state, stacked):
    it = iter(refs)
    q_ref, k_ref, v_ref, gf_ref, gb_ref, dec_ref, gn_ref = (next(it) for _ in range(7))
    if rope:
        cq, saq, sbq, ckk, sak, sbk = (next(it) for _ in range(6))
    if has_s0:
        s0_ref = next(it)
    if stacked:
        sp_ref = next(it)
    y_ref = next(it)
    if emit_state:
        s_ref = next(it)
        if stacked:
            s_ref[0] = sp_ref[...]
            s_ref = s_ref.at[1]

    qi = pl.program_id(2)
    q = q_ref[...]
    k = k_ref[...]
    if rope:
        q = _rope(q, cq[...], saq[...], sbq[...])
        k = _rope(k, ckk[...], sak[...], sbk[...])
    k = k * (DKB ** -0.5)
    kb = k.astype(bf16)
    lane = lax.broadcasted_iota(jnp.int32, (1, LANES), 1)
    tpos = (qi * tq + lax.broadcasted_iota(jnp.int32, (tq, 1), 0)).astype(f32)
    spos_row = lax.broadcasted_iota(jnp.int32, (1, l), 1).astype(f32)
    diff = tpos - spos_row
    lg = _log_sigmoid(dec_ref[...])
    gn = gn_ref[...]
    for hh in range(2):
        mh = (lane >= DKB * hh) & (lane < DKB * (hh + 1))
        qh = jnp.where(mh, q, 0.0)
        qk = _dot_nt(qh.astype(bf16), kb)
        lgf = lg[0, hh][0:1, 0:1]
        lgb = lg[1, hh][0:1, 0:1]
        df = jnp.where(diff >= 0, jnp.exp(lgf * jnp.maximum(diff, 0.0)), 0.0)
        db = jnp.where(diff <= 0, jnp.exp(lgb * jnp.maximum(-diff, 0.0)), 0.0)
        vh = v_ref[:, hh * DVB:(hh + 1) * DVB].astype(bf16)
        of = _dot((qk * df).astype(bf16), vh)
        ob = _dot((qk * db).astype(bf16), vh)
        if has_s0:
            qf = (qh * jnp.exp(lgf * (tpos + 1.0))).astype(bf16)
            qb = (qh * jnp.exp(lgb * (float(l) - tpos))).astype(bf16)
            of = of + _dot(qf, s0_ref[0].reshape(2 * DKB, DVB).astype(bf16))
            ob = ob + _dot(qb, s0_ref[1].reshape(2 * DKB, DVB).astype(bf16))
        if emit_state:
            spos = lax.broadcasted_iota(jnp.int32, (l, 1), 0).astype(f32)
            kf = (k * jnp.exp(lgf * (float(l) - 1.0 - spos))).astype(bf16)
            kw = (k * jnp.exp(lgb * spos)).astype(bf16)
            s_ref[0, hh] = _dot_tn(kf, vh)[hh * DKB:(hh + 1) * DKB, :]
            s_ref[1, hh] = _dot_tn(kw, vh)[hh * DKB:(hh + 1) * DKB, :]
        sl = slice(hh * DVB, (hh + 1) * DVB)
        y_ref[:, sl] = _silu(gf_ref[:, sl]) * _rms(of, gn) + _silu(gb_ref[:, sl]) * _rms(ob, gn)


def _retention(p3, dec_l, gn_l, *, tq, rope_tabs=None, s0=None, emit_state=False, prev_state=None):
    b, l, _ = p3.shape
    stacked = prev_state is not None
    rope = rope_tabs is not None
    has_s0 = s0 is not None
    w2 = 2 * DVB
    dec_b = jnp.broadcast_to(dec_l.astype(f32)[:, :, None, None], (2, HB, SUBLANES, LANES))
    in_specs = [
        pl.BlockSpec((None, tq, LANES), lambda i, j, qi: (i, qi, _QB // LANES + j)),
        pl.BlockSpec((None, l, LANES), lambda i, j, qi: (i, 0, _KB // LANES + j)),
        pl.BlockSpec((None, l, w2), lambda i, j, qi: (i, 0, _VB // w2 + j)),
        pl.BlockSpec((None, tq, w2), lambda i, j, qi: (i, qi, _GF // w2 + j)),
        pl.BlockSpec((None, tq, w2), lambda i, j, qi: (i, qi, _GB // w2 + j)),
        pl.BlockSpec((2, 2, SUBLANES, LANES), lambda i, j, qi: (0, j, 0, 0)),
        pl.BlockSpec((1, DVB), lambda i, j, qi: (0, 0)),
    ]
    args = [p3, p3, p3, p3, p3, dec_b, gn_l.reshape(1, DVB)]
    if rope:
        tab_q = pl.BlockSpec((tq, LANES), lambda i, j, qi: (qi, 0))
        tab_k = pl.BlockSpec((l, LANES), lambda i, j, qi: (0, 0))
        in_specs += [tab_q, tab_q, tab_q, tab_k, tab_k, tab_k]
        args += [*rope_tabs, *rope_tabs]
    state_spec = pl.BlockSpec((None, 2, 2, DKB, DVB), lambda i, j, qi: (i, 0, j, 0, 0))
    if has_s0:
        in_specs.append(state_spec)
        args.append(s0)
    if stacked:
        in_specs.append(state_spec)
        args.append(prev_state)
    out_shape = [jax.ShapeDtypeStruct((b, l, HB * DVB), f32)]
    out_specs = [pl.BlockSpec((None, tq, w2), lambda i, j, qi: (i, qi, j))]
    if emit_state:
        assert tq == l
        if stacked:
            assert DEPTH == 2
            out_shape.append(jax.ShapeDtypeStruct((b, DEPTH, 2, HB, DKB, DVB), f32))
            out_specs.append(pl.BlockSpec((None, DEPTH, 2, 2, DKB, DVB),
                                          lambda i, j, qi: (i, 0, 0, j, 0, 0)))
        else:
            out_shape.append(jax.ShapeDtypeStruct((b, 2, HB, DKB, DVB), f32))
            out_specs.append(state_spec)
    return pl.pallas_call(
        functools.partial(_ret_kernel, l=l, tq=tq, rope=rope, has_s0=has_s0, emit_state=emit_state,
                          stacked=stacked),
        out_shape=out_shape,
        grid=(b, HB // 2, l // tq),
        in_specs=in_specs,
        out_specs=out_specs,
        compiler_params=_params("arbitrary", "arbitrary", "arbitrary"),
        name="retention",
    )(*args)


def _merge_kernel(ya, yb, yc, m0, m1, m2, x_ref, g_ref, wbr, wout, o_ref):
    y = (_sigmoid(m0[...]) * _dot(ya[...].astype(bf16), wbr[0])
         + _sigmoid(m1[...]) * _dot(yb[...].astype(bf16), wbr[1])
         + _sigmoid(m2[...]) * _dot(yc[...].astype(bf16), wbr[2]))
    o_ref[...] = x_ref[...] + g_ref[...] * _dot(y.astype(bf16), wout[...])


def _merge_out(ya, yb, yc, p, x, g1, wbr, wout, rows_per_cond, *, tm):
    t, d = x.shape
    bw = ya.shape[1]
    branch = pl.BlockSpec((tm, bw), lambda i: (i, 0))
    gate = lambda n: pl.BlockSpec((tm, d), lambda i: (i, _MG // d + n))
    return pl.pallas_call(
        _merge_kernel,
        out_shape=jax.ShapeDtypeStruct((t, d), f32),
        grid=(t // tm,),
        in_specs=[
            branch, branch, branch, gate(0), gate(1), gate(2),
            pl.BlockSpec((tm, d), lambda i: (i, 0)),
            pl.BlockSpec((None, 1, d), lambda i: ((i * tm) // rows_per_cond, 0, 0)),
            pl.BlockSpec((3, bw, d), lambda i: (0, 0, 0)),
            pl.BlockSpec((d, d), lambda i: (0, 0)),
        ],
        out_specs=pl.BlockSpec((tm, d), lambda i: (i, 0)),
        compiler_params=_params("arbitrary"),
        name="merge_out",
    )(ya, yb, yc, p, p, p, x, g1, wbr, wout)


_PAIRS = [(k, j) for k in range(PEER_TOPK) for j in range(PEER_TOPK)
          if (k + 1) * (j + 1) <= PEER_TOPK]
_PAIR_VREGS = -(-len(_PAIRS) // SUBLANES)


def _allreduce_sublanes(x, op):
    for s in (4, 2, 1):
        x = op(x, pltpu.roll(x, s, 0))
    return x


def _extract_top(s3, pos3, n, payload=None):
    big = float(s3.shape[0] * SUBLANES)
    vals, sel = [], []
    for _ in range(n):
        m = _allreduce_sublanes(jnp.max(s3, axis=0), jnp.maximum)
        cand = jnp.where(s3 == m[None], pos3, big)
        p = _allreduce_sublanes(jnp.min(cand, axis=0), jnp.minimum)
        hit = cand == p[None]
        vals.append(m)
        if payload is None:
            sel.append(p)
        else:
            sel.append(_allreduce_sublanes(jnp.max(jnp.where(hit, payload, -1.0), axis=0), jnp.maximum))
        s3 = jnp.where(hit, -jnp.inf, s3)
    return vals, sel


def _pack_rows(rows, fill):
    sub = lax.broadcasted_iota(jnp.int32, (SUBLANES, LANES), 0)
    out = []
    for r in range(-(-len(rows) // SUBLANES)):
        v = jnp.full((SUBLANES, LANES), fill, f32)
        for s, x in enumerate(rows[r * SUBLANES:(r + 1) * SUBLANES]):
            v = jnp.where(sub == s, x, v)
        out.append(v)
    return jnp.stack(out, axis=0)


def _sort_network(n):
    pairs = []

    def merge(lo, hi, r):
        step = r * 2
        if step < hi - lo:
            merge(lo, hi, step)
            merge(lo + r, hi, step)
            pairs.extend((i, i + r) for i in range(lo + r, hi - r, step))
        else:
            pairs.append((lo, lo + r))

    def sort(lo, hi):
        if hi - lo >= 1:
            mid = lo + (hi - lo) // 2
            sort(lo, mid)
            sort(mid + 1, hi)
            merge(lo, hi, 1)

    sort(0, n - 1)
    return pairs


def _top_sorted_columns(s3, n):
    nrow = s3.shape[0]
    sub = lax.broadcasted_iota(jnp.int32, (SUBLANES, LANES), 0).astype(f32)
    v = [s3[r] for r in range(nrow)]
    pos = [sub + float(r * SUBLANES) for r in range(nrow)]
    for i, j in _sort_network(nrow):
        swap = (v[j] > v[i]) | ((v[j] == v[i]) & (pos[j] < pos[i]))
        v[i], v[j] = jnp.where(swap, v[j], v[i]), jnp.where(swap, v[i], v[j])
        pos[i], pos[j] = jnp.where(swap, pos[j], pos[i]), jnp.where(swap, pos[i], pos[j])
    big = float(nrow * SUBLANES)
    vals, sel = [], []
    for k in range(n):
        m = _allreduce_sublanes(v[0], jnp.maximum)
        cand = jnp.where(v[0] == m, pos[0], big)
        p = _allreduce_sublanes(cand, jnp.minimum)
        vals.append(m)
        sel.append(p)
        hit = cand == p
        for r in range(nrow - 1 - k):
            v[r] = jnp.where(hit, v[r + 1], v[r])
            pos[r] = jnp.where(hit, pos[r + 1], pos[r])
    return vals, sel


def _topk_kernel(q_ref, sk_ref, a_ref, b_ref, g_ref, at_scr, bt_scr, gt_scr):
    tt = q_ref.shape[1]
    k1 = sk_ref[0].astype(bf16)
    k2 = sk_ref[1].astype(bf16)
    nrow = PEER_NKEYS // SUBLANES
    pos_pair = (lax.broadcasted_iota(jnp.int32, (_PAIR_VREGS, SUBLANES, LANES), 0) * SUBLANES
                + lax.broadcasted_iota(jnp.int32, (_PAIR_VREGS, SUBLANES, LANES), 1)).astype(f32)

    def head(h, carry):
        for grp in range(tt // LANES):
            tok = slice(grp * LANES, (grp + 1) * LANES)
            q1 = q_ref[2 * h, tok, :].astype(bf16)
            q2 = q_ref[2 * h + 1, tok, :].astype(bf16)
            s1 = _dot_nt(k1, q1).reshape(nrow, SUBLANES, LANES)
            s2 = _dot_nt(k2, q2).reshape(nrow, SUBLANES, LANES)
            v1, i1 = _top_sorted_columns(s1, PEER_TOPK)
            v2, i2 = _top_sorted_columns(s2, PEER_TOPK)
            cand = _pack_rows([v1[k] + v2[j] for k, j in _PAIRS], -jnp.inf)
            cid = _pack_rows([i1[k] * float(PEER_NKEYS) + i2[j] for k, j in _PAIRS], -1.0)
            sc, ids = _extract_top(cand, pos_pair, PEER_TOPK, payload=cid)
            e = [jnp.exp(m - sc[0]) for m in sc]
            inv = 1.0 / functools.reduce(lambda x, y: x + y, e)
            a = [jnp.floor(i * (1.0 / PEER_NKEYS)) for i in ids]
            b = [i - x * float(PEER_NKEYS) for i, x in zip(ids, a)]
            rows = pl.ds(pl.multiple_of(h * PEER_TOPK, PEER_TOPK), PEER_TOPK)
            at_scr[rows, tok] = _pack_rows(a, 0.0).reshape(PEER_TOPK, LANES)
            bt_scr[rows, tok] = _pack_rows(b, 0.0).reshape(PEER_TOPK, LANES)
            gt_scr[rows, tok] = _pack_rows([x * inv for x in e], 0.0).reshape(PEER_TOPK, LANES)
        return carry

    lax.fori_loop(0, PEER_HEADS, head, 0)
    for grp in range(tt // LANES):
        tok = slice(grp * LANES, (grp + 1) * LANES)
        a_ref[tok, :] = at_scr[:, tok].T
        b_ref[tok, :] = bt_scr[:, tok].T
        g_ref[tok, :] = gt_scr[:, tok].T


def _peer_topk(q, subkeys, *, tt):
    nq, t, _ = q.shape
    row = pl.BlockSpec((tt, LANES), lambda i: (i, 0))
    return pl.pallas_call(
        _topk_kernel,
        out_shape=[jax.ShapeDtypeStruct((t, LANES), f32)] * 3,
        grid=(t // tt,),
        in_specs=[
            pl.BlockSpec((nq, tt, LANES), lambda i: (0, i, 0)),
            pl.BlockSpec((2, PEER_NKEYS, PEER_DKEY // 2), lambda i: (0, 0, 0)),
        ],
        out_specs=[row, row, row],
        scratch_shapes=[pltpu.VMEM((PEER_HEADS * PEER_TOPK, tt), f32)] * 3,
        compiler_params=_params("arbitrary"),
        name="peer_topk",
    )(q, subkeys)


_WGROUP = 16


def _wbuild_kernel(a_ref, b_ref, g_ref, w_ref, scr):
    tw = a_ref.shape[0]
    sub = lax.broadcasted_iota(jnp.int32, (PEER_NKEYS, LANES), 0).astype(f32)

    def group(gi, carry):
        t0 = pl.multiple_of(gi * _WGROUP, _WGROUP)
        for tt in range(_WGROUP):
            arow = a_ref[pl.ds(t0 + tt, 1), :]
            brow = b_ref[pl.ds(t0 + tt, 1), :]
            grow = g_ref[pl.ds(t0 + tt, 1), :]
            at = jnp.where(sub == arow, grow, 0.0).astype(bf16)
            bt = jnp.where(sub == brow, 1.0, 0.0).astype(bf16)
            scr[tt * _WPITCH:tt * _WPITCH + PEER_NKEYS, :] = _dot_nt(at, bt)
        for c in range(PEER_NKEYS):
            lo = scr[pl.ds(c, SUBLANES, stride=_WPITCH), :]
            hi = scr[pl.ds(SUBLANES * _WPITCH + c, SUBLANES, stride=_WPITCH), :]
            w_ref[pl.ds(t0, _WGROUP), c * LANES:(c + 1) * LANES] = (
                jnp.concatenate([lo, hi], axis=0).astype(bf16))
        return carry

    lax.fori_loop(0, tw // _WGROUP, group, 0)


def _peer_weights(a, b, g, *, tw):
    t = a.shape[0]
    row = pl.BlockSpec((tw, LANES), lambda i: (i, 0))
    return pl.pallas_call(
        _wbuild_kernel,
        out_shape=jax.ShapeDtypeStruct((t, PEER_N), bf16),
        grid=(t // tw,),
        in_specs=[row, row, row],
        out_specs=pl.BlockSpec((tw, PEER_N), lambda i: (i, 0)),
        scratch_shapes=[pltpu.VMEM((_WGROUP * _WPITCH, LANES), f32)],
        compiler_params=_params("arbitrary"),
        name="peer_weights",
    )(a, b, g)


def _gelu_tanh(x):
    return 0.5 * x * (1.0 + jnp.tanh(math.sqrt(2.0 / math.pi) * (x + 0.044715 * (x * x * x))))


_EXPERT_SUB = 256


def _expert_kernel(h_ref, u_ref, v_ref, w_ref, x_ref, g_ref, o_ref, acc):
    c = pl.program_id(1)

    @pl.when(c == 0)
    def _():
        acc[...] = jnp.zeros_like(acc)

    h = h_ref[...]
    parts = []
    for k in range(u_ref.shape[0] // _EXPERT_SUB):
        sl = slice(k * _EXPERT_SUB, (k + 1) * _EXPERT_SUB)
        a = _gelu_tanh(_dot_nt(h, u_ref[sl, :]))
        parts.append((a * w_ref[:, sl].astype(f32)).astype(bf16))
    acc[...] += _dot(jnp.concatenate(parts, axis=1), v_ref[...])

    @pl.when(c == pl.num_programs(1) - 1)
    def _():
        o_ref[...] = x_ref[...] + g_ref[...] * acc[...]


def _peer_experts(h, u, v, w, x, g2, rows_per_cond, *, tt, ec):
    t, d = x.shape
    return pl.pallas_call(
        _expert_kernel,
        out_shape=jax.ShapeDtypeStruct((t, d), f32),
        grid=(t // tt, PEER_N // ec),
        in_specs=[
            pl.BlockSpec((tt, d), lambda i, c: (i, 0)),
            pl.BlockSpec((ec, d), lambda i, c: (c, 0)),
            pl.BlockSpec((ec, d), lambda i, c: (c, 0)),
            pl.BlockSpec((tt, ec), lambda i, c: (i, c)),
            pl.BlockSpec((tt, d), lambda i, c: (i, 0)),
            pl.BlockSpec((None, 1, d), lambda i, c: ((i * tt) // rows_per_cond, 0, 0)),
        ],
        out_specs=pl.BlockSpec((tt, d), lambda i, c: (i, 0)),
        scratch_shapes=[pltpu.VMEM((tt, d), f32)],
        compiler_params=_params("arbitrary", "arbitrary"),
        name="peer_experts",
    )(h, u, v, w, x, g2)


def _final_norm_kernel(x_ref, g_ref, o_ref):
    o_ref[...] = _rms(x_ref[...], g_ref[...])


def _final_norm(x, g, *, tm):
    t, d = x.shape
    return pl.pallas_call(
        _final_norm_kernel,
        out_shape=jax.ShapeDtypeStruct((t, d), f32),
        grid=(t // tm,),
        in_specs=[pl.BlockSpec((tm, d), lambda i: (i, 0)), pl.BlockSpec((1, d), lambda i: (0, 0))],
        out_specs=pl.BlockSpec((tm, d), lambda i: (i, 0)),
        compiler_params=_params("arbitrary"),
        name="final_norm",
    )(x, g.reshape(1, d))


def _rope_tables(l):
    t = jnp.arange(l)
    rows = (t // GRID_W).astype(f32)
    cols = (t % GRID_W).astype(f32)
    half = DA // 2
    freqs = ROPE_BASE ** (-jnp.arange(0, half, 2, dtype=f32) / half)
    lane = np.arange(LANES)
    use_col = (lane % DA) >= half
    upper = (lane % half) >= half // 2
    fr = freqs[lane % (half // 2)]
    ang = jnp.where(use_col[None, :], cols[:, None], rows[:, None]) * fr[None, :]
    cos, sin = jnp.cos(ang), jnp.sin(ang)
    sa = jnp.where(upper[None, :], sin, 0.0)
    sb = jnp.where(upper[None, :], 0.0, -sin)
    return cos, sa, sb


def _split_mod(m):
    return [m[:, None, i * D_MODEL:(i + 1) * D_MODEL] for i in range(6)]


def _trunk_layer(x, nb, l, mod, cache, prev_ctx, wts, lam_init, rope_tabs):
    (norm1_l, w_in_l, lam_l, dgn_l, decay_l, rgn_l, nab_l, wbr_l, wout_l, norm2_l,
     pwq_l, psk_l, pu_l, pv_l) = wts
    t = nb * l
    rows_per_cond = t if mod.shape[0] == 1 else l
    sh1, sc1, g1, sh2, sc2, g2 = _split_mod(mod)
    p = _norm_mod_matmul(x, norm1_l, sc1, sh1, w_in_l, rows_per_cond, tm=512)[0]
    p3 = p.reshape(nb, l, IN_TOTAL)
    if cache is None:
        pk = (lambda i, j: None) if prev_ctx is None else (lambda i, j: prev_ctx[i:j])
        ya, kd, vd = _diff_attn_ctx(p3, lam_l, dgn_l, lam_init, prev=pk(0, 2))
        yc, kn, vn = _dense_attn_ctx(p3, prev=pk(2, 4))
        yb, sr = _retention(p3, decay_l, rgn_l, tq=l, emit_state=True,
                            prev_state=None if prev_ctx is None else prev_ctx[4])
        new_ctx = (kd, vd, kn, vn, sr)
    else:
        c_dk, c_dv, c_nk, c_nv, c_st = cache
        ya = _diff_attn_lat(p3, c_dk, c_dv, rope_tabs, lam_l, dgn_l, lam_init, tq=256)
        yc = _na_attn_lat(p3, c_nk, c_nv, nab_l)
        yb = _retention(p3, decay_l, rgn_l, tq=256, rope_tabs=rope_tabs, s0=c_st)[0]
        new_ctx = None
    bw = HA * 2 * DA
    x = _merge_out(ya.reshape(t, bw), yb.reshape(t, bw), yc.reshape(t, bw), p, x, g1,
                   wbr_l, wout_l, rows_per_cond, tm=256)
    q, h2 = _norm_mod_matmul(x, norm2_l, sc2, sh2, pwq_l, rows_per_cond, tm=512, emit_h=True,
                             split_out=True)
    a, b, g = _peer_topk(q, psk_l, tt=512)
    w = _peer_weights(a, b, g, tw=128)
    x = _peer_experts(h2, pu_l, pv_l, w, x, g2, rows_per_cond, tt=1024, ec=2048)
    return x, new_ctx


def kernel(x_prompt, x_sample, c, cache_diff_k, cache_diff_v, cache_na_k, cache_na_v, state_ret,
           c_ctx, w_mod, b_mod, norm1, w_in, diff_lambda, diff_gn, ret_decay, ret_gn, na_bias,
           w_branch, w_out, norm2, peer_wq, peer_subkeys, peer_u, peer_v, norm_f):
    nbp, lp, d = x_prompt.shape
    nbs, ls, _ = x_sample.shape
    n_cond = SUBLANES
    conds = jnp.concatenate([c_ctx[None, :], c, jnp.zeros((n_cond - 1 - nbs, d), f32)], axis=0)
    mod = _modulation(conds, w_mod, b_mod)
    rope_tabs = _rope_tables(ls)

    wts = [(norm1[li], _column_slabs(w_in[li], 1024), diff_lambda[li], diff_gn[li], ret_decay[li],
            ret_gn[li], na_bias[li], w_branch[li].astype(bf16), w_out[li].astype(bf16), norm2[li],
            _column_slabs(peer_wq[li], 1024), peer_subkeys[li], peer_u[li].astype(bf16),
            peer_v[li].astype(bf16)) for li in range(DEPTH)]

    xp = x_prompt.reshape(nbp * lp, d)
    ctx = None
    for li in range(DEPTH):
        lam_init = 0.8 - 0.6 * math.exp(-0.3 * li)
        xp, ctx = _trunk_layer(xp, nbp, lp, mod[li, 0:1], None, ctx, wts[li], lam_init, None)
    y_prompt = _final_norm(xp, norm_f, tm=512).reshape(nbp, lp, d)

    xs = x_sample.reshape(nbs * ls, d)
    for li in range(DEPTH):
        lam_init = 0.8 - 0.6 * math.exp(-0.3 * li)
        cache = (cache_diff_k[:, li], cache_diff_v[:, li], cache_na_k[:, li], cache_na_v[:, li],
                 state_ret[:, li])
        xs, _ = _trunk_layer(xs, nbs, ls, mod[li, 1:1 + nbs], cache, None, wts[li], lam_init,
                             rope_tabs)
    y_sample = _final_norm(xs, norm_f, tm=512).reshape(nbs, ls, d)
    return (y_prompt, y_sample, *ctx)
```

```python
import functools
import math

import numpy as np
import jax
import jax.numpy as jnp
from jax import lax
from jax.experimental import pallas as pl
from jax.experimental.pallas import tpu as pltpu

D_MODEL = 1024
DEPTH = 2
GRID_W = 64
HA = 4
DA = 64
HB = 4
DKB = 64
DVB = 128
HC = 4
DC = 128
NA_ROWS = 8
NA_COLS = 16
PEER_HEADS = 8
PEER_NKEYS = 128
PEER_DKEY = 256
PEER_TOPK = 16
PEER_N = PEER_NKEYS * PEER_NKEYS
ROPE_BASE = 10000.0
EPS = 1e-6

_QA, _KA, _VA = 0, 512, 1024
_QB, _KB, _VB, _GF, _GB = 1536, 1792, 2048, 2560, 3072
_QC, _KC, _VC, _MG = 3584, 4096, 4608, 5120
IN_TOTAL = 8192

LANES = 128
SUBLANES = 8
MASK_VALUE = -1e30
_WPITCH = 136

f32 = jnp.float32
bf16 = jnp.bfloat16


def _dot(a, b):
    return jnp.dot(a, b, preferred_element_type=f32)


def _dot_nt(a, b):
    return lax.dot_general(a, b, (((1,), (1,)), ((), ())), preferred_element_type=f32)


def _dot_tn(a, b):
    return lax.dot_general(a, b, (((0,), (0,)), ((), ())), preferred_element_type=f32)


def _sigmoid(x):
    return 1.0 / (1.0 + jnp.exp(-x))


def _silu(x):
    return x * _sigmoid(x)


def _rms(x, g):
    return x * lax.rsqrt(jnp.mean(x * x, axis=-1, keepdims=True) + EPS) * g


def _softmax(s):
    e = jnp.exp(s - jnp.max(s, axis=-1, keepdims=True))
    return e * (1.0 / jnp.sum(e, axis=-1, keepdims=True))


def _rope(x, c, sa, sb):
    return x * c + pltpu.roll(x, 16, 1) * sa + pltpu.roll(x, LANES - 16, 1) * sb


def _params(*sem):
    return pltpu.CompilerParams(dimension_semantics=sem)


def _mod_kernel(c_ref, w_ref, b_ref, o_ref):
    a = _silu(c_ref[...]).astype(bf16)
    o_ref[...] = _dot(a, w_ref[...].astype(bf16)) + b_ref[...]


def _modulation(conds, w_mod, b_mod):
    nc = conds.shape[0]
    n = w_mod.shape[-1]
    tn = 1536
    return pl.pallas_call(
        _mod_kernel,
        out_shape=jax.ShapeDtypeStruct((DEPTH, nc, n), f32),
        grid=(DEPTH, n // tn),
        in_specs=[
            pl.BlockSpec((nc, D_MODEL), lambda l, j: (0, 0)),
            pl.BlockSpec((None, D_MODEL, tn), lambda l, j: (l, 0, j)),
            pl.BlockSpec((None, 1, tn), lambda l, j: (l, 0, j)),
        ],
        out_specs=pl.BlockSpec((None, nc, tn), lambda l, j: (l, 0, j)),
        compiler_params=_params("arbitrary", "arbitrary"),
        name="modulation",
    )(conds, w_mod, b_mod.reshape(DEPTH, 1, n))


def _nmm_kernel(x_ref, g_ref, sc_ref, sh_ref, w_ref, o_ref, *rest, emit_h, split_out):
    h_scr = rest[-1]
    j = pl.program_id(1)

    @pl.when(j == 0)
    def _():
        h = _rms(x_ref[...], g_ref[...]) * (1.0 + sc_ref[...]) + sh_ref[...]
        h_scr[...] = h.astype(bf16)
        if emit_h:
            rest[0][...] = h.astype(bf16)

    res = _dot(h_scr[...], w_ref[j])
    if split_out:
        for c in range(res.shape[1] // LANES):
            o_ref[c] = res[:, c * LANES:(c + 1) * LANES]
    else:
        o_ref[...] = res


def _column_slabs(w, tn):
    d, n = w.shape
    return w.reshape(d, n // tn, tn).transpose(1, 0, 2).astype(bf16)


def _norm_mod_matmul(x, gain, sc, sh, w_slabs, rows_per_cond, *, tm, emit_h=False, split_out=False):
    t, d = x.shape
    nslab, _, tn = w_slabs.shape
    n = nslab * tn
    cond_map = lambda i, j: ((i * tm) // rows_per_cond, 0, 0)
    if split_out:
        out_shape = [jax.ShapeDtypeStruct((n // LANES, t, LANES), f32)]
        out_specs = [pl.BlockSpec((tn // LANES, tm, LANES), lambda i, j: (j, i, 0))]
    else:
        out_shape = [jax.ShapeDtypeStruct((t, n), f32)]
        out_specs = [pl.BlockSpec((tm, tn), lambda i, j: (i, j))]
    if emit_h:
        out_shape.append(jax.ShapeDtypeStruct((t, d), bf16))
        out_specs.append(pl.BlockSpec((tm, d), lambda i, j: (i, 0)))
    return pl.pallas_call(
        functools.partial(_nmm_kernel, emit_h=emit_h, split_out=split_out),
        out_shape=out_shape,
        grid=(t // tm, nslab),
        in_specs=[
            pl.BlockSpec((tm, d), lambda i, j: (i, 0)),
            pl.BlockSpec((1, d), lambda i, j: (0, 0)),
            pl.BlockSpec((None, 1, d), cond_map),
            pl.BlockSpec((None, 1, d), cond_map),
            pl.BlockSpec((nslab, d, tn), lambda i, j: (0, 0, 0), pipeline_mode=pl.Buffered(1)),
        ],
        out_specs=out_specs,
        scratch_shapes=[pltpu.VMEM((tm, d), bf16)],
        compiler_params=_params("arbitrary", "arbitrary"),
        name="norm_mod_matmul",
    )(x, gain.reshape(1, d), sc, sh, w_slabs)


def _lambda(lam_ref, lam_init):
    l = lam_ref[...]
    a = jnp.sum(l[0:1] * l[1:2], axis=-1, keepdims=True)
    b = jnp.sum(l[2:3] * l[3:4], axis=-1, keepdims=True)
    return jnp.exp(a) - jnp.exp(b) + lam_init


def _diff_core(q, kb, vb, lam_ref, gn_ref, lam_init):
    lane = lax.broadcasted_iota(jnp.int32, (1, LANES), 1)
    lo = lane < DA
    q1 = jnp.where(lo, q, 0.0).astype(bf16)
    q2 = jnp.where(lo, 0.0, q).astype(bf16)
    scale = DA ** -0.5
    p1 = _softmax(_dot_nt(q1, kb) * scale).astype(bf16)
    p2 = _softmax(_dot_nt(q2, kb) * scale).astype(bf16)
    o = _dot(p1, vb) - _lambda(lam_ref, lam_init) * _dot(p2, vb)
    return _rms(o, gn_ref[...]) * (1.0 - lam_init)


def _emit_cache(out_ref, prev_ref, cur, h):
    if prev_ref is None:
        out_ref[h] = cur
    else:
        out_ref[0, h] = prev_ref[h]
        out_ref[1, h] = cur


def _diff_ctx_kernel(*refs, lam_init, stacked):
    if stacked:
        q_ref, k_ref, v_ref, kp_ref, vp_ref, lam_ref, gn_ref, o_ref, ko_ref, vo_ref = refs
    else:
        q_ref, k_ref, v_ref, lam_ref, gn_ref, o_ref, ko_ref, vo_ref = refs
        kp_ref = vp_ref = None
    for h in range(HA):
        sl = slice(h * LANES, (h + 1) * LANES)
        k = k_ref[:, sl]
        v = v_ref[:, sl]
        _emit_cache(ko_ref, kp_ref, k, h)
        _emit_cache(vo_ref, vp_ref, v, h)
        o_ref[:, sl] = _diff_core(q_ref[:, sl], k.astype(bf16), v.astype(bf16), lam_ref, gn_ref,
                                  lam_init)


def _head_cache_specs(b, nh, l, dh, prev):
    one = pl.BlockSpec((None, nh, l, dh), lambda i: (i, 0, 0, 0))
    if prev is None:
        return [], jax.ShapeDtypeStruct((b, nh, l, dh), f32), one
    assert DEPTH == 2
    return ([one], jax.ShapeDtypeStruct((b, DEPTH, nh, l, dh), f32),
            pl.BlockSpec((None, DEPTH, nh, l, dh), lambda i: (i, 0, 0, 0, 0)))


def _diff_attn_ctx(p3, lam_l, gn_l, lam_init, prev=None):
    b, l, _ = p3.shape
    w = HA * 2 * DA
    col = lambda off: (lambda i: (i, 0, off // w))
    prev_specs, cache_shape, cache_spec = _head_cache_specs(b, HA, l, 2 * DA, prev)
    return pl.pallas_call(
        functools.partial(_diff_ctx_kernel, lam_init=lam_init, stacked=prev is not None),
        out_shape=[jax.ShapeDtypeStruct((b, l, w), f32), cache_shape, cache_shape],
        grid=(b,),
        in_specs=[
            pl.BlockSpec((None, l, w), col(_QA)),
            pl.BlockSpec((None, l, w), col(_KA)),
            pl.BlockSpec((None, l, w), col(_VA)),
            *prev_specs, *prev_specs,
            pl.BlockSpec((4, DA), lambda i: (0, 0)),
            pl.BlockSpec((1, 2 * DA), lambda i: (0, 0)),
        ],
        out_specs=[pl.BlockSpec((None, l, w), lambda i: (i, 0, 0)), cache_spec, cache_spec],
        compiler_params=_params("arbitrary"),
        name="diff_attn_ctx",
    )(p3, p3, p3, *(prev or ()), lam_l, gn_l.reshape(1, 2 * DA))


def _diff_lat_kernel(q_ref, k_ref, v_ref, ck_ref, cv_ref, cq, saq, sbq, ckk, sak, sbk,
                     lam_ref, gn_ref, o_ref, k_scr, v_scr, *, lam_init, past):
    @pl.when(pl.program_id(2) == 0)
    def _():
        k_scr[0:past, :] = ck_ref[...].astype(bf16)
        k_scr[past:, :] = _rope(k_ref[...], ckk[...], sak[...], sbk[...]).astype(bf16)
        v_scr[0:past, :] = cv_ref[...].astype(bf16)
        v_scr[past:, :] = v_ref[...].astype(bf16)

    q = _rope(q_ref[...], cq[...], saq[...], sbq[...])
    o_ref[...] = _diff_core(q, k_scr[...], v_scr[...], lam_ref, gn_ref, lam_init)


def _diff_attn_lat(p3, c_dk, c_dv, rope_tabs, lam_l, gn_l, lam_init, *, tq):
    b, l, _ = p3.shape
    past = c_dk.shape[2]
    col = lambda off: (lambda i, h, qi: (i, 0, off // LANES + h))
    cache = pl.BlockSpec((None, None, past, LANES), lambda i, h, qi: (i, h, 0, 0))
    tab_q = pl.BlockSpec((tq, LANES), lambda i, h, qi: (qi, 0))
    tab_k = pl.BlockSpec((l, LANES), lambda i, h, qi: (0, 0))
    return pl.pallas_call(
        functools.partial(_diff_lat_kernel, lam_init=lam_init, past=past),
        out_shape=jax.ShapeDtypeStruct((b, l, HA * 2 * DA), f32),
        grid=(b, HA, l // tq),
        in_specs=[
            pl.BlockSpec((None, tq, LANES), lambda i, h, qi: (i, qi, _QA // LANES + h)),
            pl.BlockSpec((None, l, LANES), col(_KA)),
            pl.BlockSpec((None, l, LANES), col(_VA)),
            cache, cache, tab_q, tab_q, tab_q, tab_k, tab_k, tab_k,
            pl.BlockSpec((4, DA), lambda i, h, qi: (0, 0)),
            pl.BlockSpec((1, 2 * DA), lambda i, h, qi: (0, 0)),
        ],
        out_specs=pl.BlockSpec((None, tq, LANES), lambda i, h, qi: (i, qi, h)),
        scratch_shapes=[pltpu.VMEM((past + l, LANES), bf16), pltpu.VMEM((past + l, LANES), bf16)],
        compiler_params=_params("arbitrary", "arbitrary", "arbitrary"),
        name="diff_attn_lat",
    )(p3, p3, p3, c_dk, c_dv, *rope_tabs, *rope_tabs, lam_l, gn_l.reshape(1, 2 * DA))


def _dense_attn_kernel(*refs, stacked):
    if stacked:
        q_ref, k_ref, v_ref, kp_ref, vp_ref, o_ref, ko_ref, vo_ref = refs
    else:
        q_ref, k_ref, v_ref, o_ref, ko_ref, vo_ref = refs
        kp_ref = vp_ref = None
    for h in range(HC):
        sl = slice(h * DC, (h + 1) * DC)
        k = k_ref[:, sl]
        v = v_ref[:, sl]
        _emit_cache(ko_ref, kp_ref, k, h)
        _emit_cache(vo_ref, vp_ref, v, h)
        s = _dot_nt(q_ref[:, sl].astype(bf16), k.astype(bf16)) * (DC ** -0.5)
        o_ref[:, sl] = _dot(_softmax(s).astype(bf16), v.astype(bf16))


def _dense_attn_ctx(p3, prev=None):
    b, l, _ = p3.shape
    w = HC * DC
    col = lambda off: (lambda i: (i, 0, off // w))
    prev_specs, cache_shape, cache_spec = _head_cache_specs(b, HC, l, DC, prev)
    return pl.pallas_call(
        functools.partial(_dense_attn_kernel, stacked=prev is not None),
        out_shape=[jax.ShapeDtypeStruct((b, l, w), f32), cache_shape, cache_shape],
        grid=(b,),
        in_specs=[
            pl.BlockSpec((None, l, w), col(_QC)),
            pl.BlockSpec((None, l, w), col(_KC)),
            pl.BlockSpec((None, l, w), col(_VC)),
            *prev_specs, *prev_specs,
        ],
        out_specs=[pl.BlockSpec((None, l, w), lambda i: (i, 0, 0)), cache_spec, cache_spec],
        compiler_params=_params("arbitrary"),
        name="dense_attn_ctx",
    )(p3, p3, p3, *(prev or ()))


def _na_kernel(q_ref, k_ref, v_ref, ck_ref, cv_ref, e_ref, o_ref, tl_scr, tr_scr, *, rows, kr):
    scale = DC ** -0.5
    ckb = ck_ref[...].astype(bf16)
    cvb = cv_ref[...].astype(bf16)
    qcol = lax.broadcasted_iota(jnp.int32, (GRID_W, LANES), 0)
    lane = lax.broadcasted_iota(jnp.int32, (GRID_W, LANES), 1)
    kcol = lax.bitwise_and(lane, GRID_W - 1)
    cs = jnp.clip(qcol - NA_COLS // 2, 0, GRID_W - NA_COLS)
    col_ok = (kcol >= cs) & (kcol < cs + NA_COLS)
    for ro in range(2 * NA_ROWS - 1):
        t = pltpu.roll(jnp.broadcast_to(e_ref[ro:ro + 1, :], (GRID_W, LANES)), 0, 1,
                       stride=1, stride_axis=0)
        tl_scr[ro] = t
        tr_scr[ro] = pltpu.roll(t, GRID_W, 1)
    for r in range(rows):
        rs = min(max(r - kr // 2, 0), rows - kr)
        base = rs - r + NA_ROWS - 1
        bias = jnp.concatenate(
            [jnp.where(col_ok, jnp.where(lane < GRID_W, tl_scr[base + 2 * m], tr_scr[base + 2 * m + 1]),
                       MASK_VALUE) for m in range(kr // 2)], axis=1)
        qr = q_ref[r * GRID_W:(r + 1) * GRID_W, :].astype(bf16)
        kl = k_ref[rs * GRID_W:(rs + kr) * GRID_W, :].astype(bf16)
        vl = v_ref[rs * GRID_W:(rs + kr) * GRID_W, :].astype(bf16)
        sl = _dot_nt(qr, kl) * scale + bias
        sc = _dot_nt(qr, ckb) * scale
        m = jnp.maximum(jnp.max(sl, axis=-1, keepdims=True), jnp.max(sc, axis=-1, keepdims=True))
        el = jnp.exp(sl - m)
        ec = jnp.exp(sc - m)
        inv = 1.0 / (jnp.sum(el, axis=-1, keepdims=True) + jnp.sum(ec, axis=-1, keepdims=True))
        pl_ = (el * inv).astype(bf16)
        pc_ = (ec * inv).astype(bf16)
        o_ref[r * GRID_W:(r + 1) * GRID_W, :] = _dot(pl_, vl) + _dot(pc_, cvb)


def _na_bias_rows(table):
    nc = 2 * NA_COLS - 1
    first, last = table[..., :1], table[..., nc - 1:]
    return jnp.concatenate(
        [table[..., NA_COLS - 1:],
         jnp.broadcast_to(last, (*table.shape[:-1], GRID_W + 1 - NA_COLS)),
         jnp.broadcast_to(first, (*table.shape[:-1], GRID_W - NA_COLS)),
         table[..., :NA_COLS - 1]], axis=-1).astype(f32)


def _na_attn_lat(p3, c_nk, c_nv, table):
    b, l, _ = p3.shape
    past = c_nk.shape[2]
    rows = l // GRID_W
    kr = min(NA_ROWS, rows)
    assert kr % 2 == 0 and 2 * GRID_W == LANES
    nro = 2 * NA_ROWS - 1
    col = lambda off: (lambda i, h: (i, 0, off // LANES + h))
    cache = pl.BlockSpec((None, None, past, LANES), lambda i, h: (i, h, 0, 0))
    return pl.pallas_call(
        functools.partial(_na_kernel, rows=rows, kr=kr),
        out_shape=jax.ShapeDtypeStruct((b, l, HC * DC), f32),
        grid=(b, HC),
        in_specs=[
            pl.BlockSpec((None, l, LANES), col(_QC)),
            pl.BlockSpec((None, l, LANES), col(_KC)),
            pl.BlockSpec((None, l, LANES), col(_VC)),
            cache, cache,
            pl.BlockSpec((None, nro, LANES), lambda i, h: (h, 0, 0)),
        ],
        out_specs=pl.BlockSpec((None, l, LANES), lambda i, h: (i, 0, h)),
        scratch_shapes=[pltpu.VMEM((nro, GRID_W, LANES), f32)] * 2,
        compiler_params=_params("arbitrary", "arbitrary"),
        name="na_attn_lat",
    )(p3, p3, p3, c_nk, c_nv, _na_bias_rows(table))


def _log_sigmoid(x):
    return -(jnp.maximum(-x, 0.0) + jnp.log1p(jnp.exp(-jnp.abs(x))))


def _ret_kernel(*refs, l, tq, rope, has_s0, emit_state, stacked):
    it = iter(refs)
    q_ref, k_ref, v_ref, gf_ref, gb_ref, dec_ref, gn_ref = (next(it) for _ in range(7))
    if rope:
        cq, saq, sbq, ckk, sak, sbk = (next(it) for _ in range(6))
    if has_s0:
        s0_ref = next(it)
    if stacked:
        sp_ref = next(it)
    y_ref = next(it)
    if emit_state:
        s_ref = next(it)
        if stacked:
            s_ref[0] = sp_ref[...]
            s_ref = s_ref.at[1]

    qi = pl.program_id(2)
    q = q_ref[...]
    k = k_ref[...]
    if rope:
        q = _rope(q, cq[...], saq[...], sbq[...])
        k = _rope(k, ckk[...], sak[...], sbk[...])
    k = k * (DKB ** -0.5)
    kb = k.astype(bf16)
    lane = lax.broadcasted_iota(jnp.int32, (1, LANES), 1)
    tpos = (qi * tq + lax.broadcasted_iota(jnp.int32, (tq, 1), 0)).astype(f32)
    spos_row = lax.broadcasted_iota(jnp.int32, (1, l), 1).astype(f32)
    diff = tpos - spos_row
    lg = _log_sigmoid(dec_ref[...])
    gn = gn_ref[...]
    for hh in range(2):
        mh = (lane >= DKB * hh) & (lane < DKB * (hh + 1))
        qh = jnp.where(mh, q, 0.0)
        qk = _dot_nt(qh.astype(bf16), kb)
        lgf = lg[0, hh][0:1, 0:1]
        lgb = lg[1, hh][0:1, 0:1]
        df = jnp.where(diff >= 0, jnp.exp(lgf * jnp.maximum(diff, 0.0)), 0.0)
        db = jnp.where(diff <= 0, jnp.exp(lgb * jnp.maximum(-diff, 0.0)), 0.0)
        vh = v_ref[:, hh * DVB:(hh + 1) * DVB].astype(bf16)
        of = _dot((qk * df).astype(bf16), vh)
        ob = _dot((qk * db).astype(bf16), vh)
        if has_s0:
            qf = (qh * jnp.exp(lgf * (tpos + 1.0))).astype(bf16)
            qb = (qh * jnp.exp(lgb * (float(l) - tpos))).astype(bf16)
            of = of + _dot(qf, s0_ref[0].reshape(2 * DKB, DVB).astype(bf16))
            ob = ob + _dot(qb, s0_ref[1].reshape(2 * DKB, DVB).astype(bf16))
        if emit_state:
            spos = lax.broadcasted_iota(jnp.int32, (l, 1), 0).astype(f32)
            kf = (k * jnp.exp(lgf * (float(l) - 1.0 - spos))).astype(bf16)
            kw = (k * jnp.exp(lgb * spos)).astype(bf16)
            s_ref[0, hh] = _dot_tn(kf, vh)[hh * DKB:(hh + 1) * DKB, :]
            s_ref[1, hh] = _dot_tn(kw, vh)[hh * DKB:(hh + 1) * DKB, :]
        sl = slice(hh * DVB, (hh + 1) * DVB)
        y_ref[:, sl] = _silu(gf_ref[:, sl]) * _rms(of, gn) + _silu(gb_ref[:, sl]) * _rms(ob, gn)


def _retention(p3, dec_l, gn_l, *, tq, rope_tabs=None, s0=None, emit_state=False, prev_state=None):
    b, l, _ = p3.shape
    stacked = prev_state is not None
    rope = rope_tabs is not None
    has_s0 = s0 is not None
    w2 = 2 * DVB
    dec_b = jnp.broadcast_to(dec_l.astype(f32)[:, :, None, None], (2, HB, SUBLANES, LANES))
    in_specs = [
        pl.BlockSpec((None, tq, LANES), lambda i, j, qi: (i, qi, _QB // LANES + j)),
        pl.BlockSpec((None, l, LANES), lambda i, j, qi: (i, 0, _KB // LANES + j)),
        pl.BlockSpec((None, l, w2), lambda i, j, qi: (i, 0, _VB // w2 + j)),
        pl.BlockSpec((None, tq, w2), lambda i, j, qi: (i, qi, _GF // w2 + j)),
        pl.BlockSpec((None, tq, w2), lambda i, j, qi: (i, qi, _GB // w2 + j)),
        pl.BlockSpec((2, 2, SUBLANES, LANES), lambda i, j, qi: (0, j, 0, 0)),
        pl.BlockSpec((1, DVB), lambda i, j, qi: (0, 0)),
    ]
    args = [p3, p3, p3, p3, p3, dec_b, gn_l.reshape(1, DVB)]
    if rope:
        tab_q = pl.BlockSpec((tq, LANES), lambda i, j, qi: (qi, 0))
        tab_k = pl.BlockSpec((l, LANES), lambda i, j, qi: (0, 0))
        in_specs += [tab_q, tab_q, tab_q, tab_k, tab_k, tab_k]
        args += [*rope_tabs, *rope_tabs]
    state_spec = pl.BlockSpec((None, 2, 2, DKB, DVB), lambda i, j, qi: (i, 0, j, 0, 0))
    if has_s0:
        in_specs.append(state_spec)
        args.append(s0)
    if stacked:
        in_specs.append(state_spec)
        args.append(prev_state)
    out_shape = [jax.ShapeDtypeStruct((b, l, HB * DVB), f32)]
    out_specs = [pl.BlockSpec((None, tq, w2), lambda i, j, qi: (i, qi, j))]
    if emit_state:
        assert tq == l
        if stacked:
            assert DEPTH == 2
            out_shape.append(jax.ShapeDtypeStruct((b, DEPTH, 2, HB, DKB, DVB), f32))
            out_specs.append(pl.BlockSpec((None, DEPTH, 2, 2, DKB, DVB),
                                          lambda i, j, qi: (i, 0, 0, j, 0, 0)))
        else:
            out_shape.append(jax.ShapeDtypeStruct((b, 2, HB, DKB, DVB), f32))
            out_specs.append(state_spec)
    return pl.pallas_call(
        functools.partial(_ret_kernel, l=l, tq=tq, rope=rope, has_s0=has_s0, emit_state=emit_state,
                          stacked=stacked),
        out_shape=out_shape,
        grid=(b, HB // 2, l // tq),
        in_specs=in_specs,
        out_specs=out_specs,
        compiler_params=_params("arbitrary", "arbitrary", "arbitrary"),
        name="retention",
    )(*args)


def _merge_kernel(ya, yb, yc, m0, m1, m2, x_ref, g_ref, wbr, wout, o_ref):
    y = (_sigmoid(m0[...]) * _dot(ya[...].astype(bf16), wbr[0])
         + _sigmoid(m1[...]) * _dot(yb[...].astype(bf16), wbr[1])
         + _sigmoid(m2[...]) * _dot(yc[...].astype(bf16), wbr[2]))
    o_ref[...] = x_ref[...] + g_ref[...] * _dot(y.astype(bf16), wout[...])


def _merge_out(ya, yb, yc, p, x, g1, wbr, wout, rows_per_cond, *, tm):
    t, d = x.shape
    bw = ya.shape[1]
    branch = pl.BlockSpec((tm, bw), lambda i: (i, 0))
    gate = lambda n: pl.BlockSpec((tm, d), lambda i: (i, _MG // d + n))
    return pl.pallas_call(
        _merge_kernel,
        out_shape=jax.ShapeDtypeStruct((t, d), f32),
        grid=(t // tm,),
        in_specs=[
            branch, branch, branch, gate(0), gate(1), gate(2),
            pl.BlockSpec((tm, d), lambda i: (i, 0)),
            pl.BlockSpec((None, 1, d), lambda i: ((i * tm) // rows_per_cond, 0, 0)),
            pl.BlockSpec((3, bw, d), lambda i: (0, 0, 0)),
            pl.BlockSpec((d, d), lambda i: (0, 0)),
        ],
        out_specs=pl.BlockSpec((tm, d), lambda i: (i, 0)),
        compiler_params=_params("arbitrary"),
        name="merge_out",
    )(ya, yb, yc, p, p, p, x, g1, wbr, wout)


_PAIRS = [(k, j) for k in range(PEER_TOPK) for j in range(PEER_TOPK)
          if (k + 1) * (j + 1) <= PEER_TOPK]
_PAIR_VREGS = -(-len(_PAIRS) // SUBLANES)


def _allreduce_sublanes(x, op):
    for s in (4, 2, 1):
        x = op(x, pltpu.roll(x, s, 0))
    return x


def _extract_top(s3, pos3, n, payload=None):
    big = float(s3.shape[0] * SUBLANES)
    vals, sel = [], []
    for _ in range(n):
        m = _allreduce_sublanes(jnp.max(s3, axis=0), jnp.maximum)
        cand = jnp.where(s3 == m[None], pos3, big)
        p = _allreduce_sublanes(jnp.min(cand, axis=0), jnp.minimum)
        hit = cand == p[None]
        vals.append(m)
        if payload is None:
            sel.append(p)
        else:
            sel.append(_allreduce_sublanes(jnp.max(jnp.where(hit, payload, -1.0), axis=0), jnp.maximum))
        s3 = jnp.where(hit, -jnp.inf, s3)
    return vals, sel


def _pack_rows(rows, fill):
    sub = lax.broadcasted_iota(jnp.int32, (SUBLANES, LANES), 0)
    out = []
    for r in range(-(-len(rows) // SUBLANES)):
        v = jnp.full((SUBLANES, LANES), fill, f32)
        for s, x in enumerate(rows[r * SUBLANES:(r + 1) * SUBLANES]):
            v = jnp.where(sub == s, x, v)
        out.append(v)
    return jnp.stack(out, axis=0)


def _sort_network(n):
    pairs = []

    def merge(lo, hi, r):
        step = r * 2
        if step < hi - lo:
            merge(lo, hi, step)
            merge(lo + r, hi, step)
            pairs.extend((i, i + r) for i in range(lo + r, hi - r, step))
        else:
            pairs.append((lo, lo + r))

    def sort(lo, hi):
        if hi - lo >= 1:
            mid = lo + (hi - lo) // 2
            sort(lo, mid)
            sort(mid + 1, hi)
            merge(lo, hi, 1)

    sort(0, n - 1)
    return pairs


def _top_sorted_columns(s3, n):
    nrow = s3.shape[0]
    sub = lax.broadcasted_iota(jnp.int32, (SUBLANES, LANES), 0).astype(f32)
    v = [s3[r] for r in range(nrow)]
    pos = [sub + float(r * SUBLANES) for r in range(nrow)]
    for i, j in _sort_network(nrow):
        swap = (v[j] > v[i]) | ((v[j] == v[i]) & (pos[j] < pos[i]))
        v[i], v[j] = jnp.where(swap, v[j], v[i]), jnp.where(swap, v[i], v[j])
        pos[i], pos[j] = jnp.where(swap, pos[j], pos[i]), jnp.where(swap, pos[i], pos[j])
    big = float(nrow * SUBLANES)
    vals, sel = [], []
    for k in range(n):
        m = _allreduce_sublanes(v[0], jnp.maximum)
        cand = jnp.where(v[0] == m, pos[0], big)
        p = _allreduce_sublanes(cand, jnp.minimum)
        vals.append(m)
        sel.append(p)
        hit = cand == p
        for r in range(nrow - 1 - k):
            v[r] = jnp.where(hit, v[r + 1], v[r])
            pos[r] = jnp.where(hit, pos[r + 1], pos[r])
    return vals, sel


def _topk_kernel(q_ref, sk_ref, a_ref, b_ref, g_ref, at_scr, bt_scr, gt_scr):
    tt = q_ref.shape[1]
    k1 = sk_ref[0].astype(bf16)
    k2 = sk_ref[1].astype(bf16)
    nrow = PEER_NKEYS // SUBLANES
    pos_pair = (lax.broadcasted_iota(jnp.int32, (_PAIR_VREGS, SUBLANES, LANES), 0) * SUBLANES
                + lax.broadcasted_iota(jnp.int32, (_PAIR_VREGS, SUBLANES, LANES), 1)).astype(f32)

    def head(h, carry):
        for grp in range(tt // LANES):
            tok = slice(grp * LANES, (grp + 1) * LANES)
            q1 = q_ref[2 * h, tok, :].astype(bf16)
            q2 = q_ref[2 * h + 1, tok, :].astype(bf16)
            s1 = _dot_nt(k1, q1).reshape(nrow, SUBLANES, LANES)
            s2 = _dot_nt(k2, q2).reshape(nrow, SUBLANES, LANES)
            v1, i1 = _top_sorted_columns(s1, PEER_TOPK)
            v2, i2 = _top_sorted_columns(s2, PEER_TOPK)
            cand = _pack_rows([v1[k] + v2[j] for k, j in _PAIRS], -jnp.inf)
            cid = _pack_rows([i1[k] * float(PEER_NKEYS) + i2[j] for k, j in _PAIRS], -1.0)
            sc, ids = _extract_top(cand, pos_pair, PEER_TOPK, payload=cid)
            e = [jnp.exp(m - sc[0]) for m in sc]
            inv = 1.0 / functools.reduce(lambda x, y: x + y, e)
            a = [jnp.floor(i * (1.0 / PEER_NKEYS)) for i in ids]
            b = [i - x * float(PEER_NKEYS) for i, x in zip(ids, a)]
            rows = pl.ds(pl.multiple_of(h * PEER_TOPK, PEER_TOPK), PEER_TOPK)
            at_scr[rows, tok] = _pack_rows(a, 0.0).reshape(PEER_TOPK, LANES)
            bt_scr[rows, tok] = _pack_rows(b, 0.0).reshape(PEER_TOPK, LANES)
            gt_scr[rows, tok] = _pack_rows([x * inv for x in e], 0.0).reshape(PEER_TOPK, LANES)
        return carry

    lax.fori_loop(0, PEER_HEADS, head, 0)
    for grp in range(tt // LANES):
        tok = slice(grp * LANES, (grp + 1) * LANES)
        a_ref[tok, :] = at_scr[:, tok].T
        b_ref[tok, :] = bt_scr[:, tok].T
        g_ref[tok, :] = gt_scr[:, tok].T


def _peer_topk(q, subkeys, *, tt):
    nq, t, _ = q.shape
    row = pl.BlockSpec((tt, LANES), lambda i: (i, 0))
    return pl.pallas_call(
        _topk_kernel,
        out_shape=[jax.ShapeDtypeStruct((t, LANES), f32)] * 3,
        grid=(t // tt,),
        in_specs=[
            pl.BlockSpec((nq, tt, LANES), lambda i: (0, i, 0)),
            pl.BlockSpec((2, PEER_NKEYS, PEER_DKEY // 2), lambda i: (0, 0, 0)),
        ],
        out_specs=[row, row, row],
        scratch_shapes=[pltpu.VMEM((PEER_HEADS * PEER_TOPK, tt), f32)] * 3,
        compiler_params=_params("arbitrary"),
        name="peer_topk",
    )(q, subkeys)


_WGROUP = 16


def _wbuild_kernel(a_ref, b_ref, g_ref, w_ref, scr):
    tw = a_ref.shape[0]
    sub = lax.broadcasted_iota(jnp.int32, (PEER_NKEYS, LANES), 0).astype(f32)

    def group(gi, carry):
        t0 = pl.multiple_of(gi * _WGROUP, _WGROUP)
        for tt in range(_WGROUP):
            arow = a_ref[pl.ds(t0 + tt, 1), :]
            brow = b_ref[pl.ds(t0 + tt, 1), :]
            grow = g_ref[pl.ds(t0 + tt, 1), :]
            at = jnp.where(sub == arow, grow, 0.0).astype(bf16)
            bt = jnp.where(sub == brow, 1.0, 0.0).astype(bf16)
            scr[tt * _WPITCH:tt * _WPITCH + PEER_NKEYS, :] = _dot_nt(at, bt)
        for c in range(PEER_NKEYS):
            lo = scr[pl.ds(c, SUBLANES, stride=_WPITCH), :]
            hi = scr[pl.ds(SUBLANES * _WPITCH + c, SUBLANES, stride=_WPITCH), :]
            w_ref[pl.ds(t0, _WGROUP), c * LANES:(c + 1) * LANES] = (
                jnp.concatenate([lo, hi], axis=0).astype(bf16))
        return carry

    lax.fori_loop(0, tw // _WGROUP, group, 0)


def _peer_weights(a, b, g, *, tw):
    t = a.shape[0]
    row = pl.BlockSpec((tw, LANES), lambda i: (i, 0))
    return pl.pallas_call(
        _wbuild_kernel,
        out_shape=jax.ShapeDtypeStruct((t, PEER_N), bf16),
        grid=(t // tw,),
        in_specs=[row, row, row],
        out_specs=pl.BlockSpec((tw, PEER_N), lambda i: (i, 0)),
        scratch_shapes=[pltpu.VMEM((_WGROUP * _WPITCH, LANES), f32)],
        compiler_params=_params("arbitrary"),
        name="peer_weights",
    )(a, b, g)


def _gelu_tanh(x):
    return 0.5 * x * (1.0 + jnp.tanh(math.sqrt(2.0 / math.pi) * (x + 0.044715 * (x * x * x))))


_EXPERT_SUB = 256


def _expert_kernel(h_ref, u_ref, v_ref, w_ref, x_ref, g_ref, o_ref, acc):
    c = pl.program_id(1)

    @pl.when(c == 0)
    def _():
        acc[...] = jnp.zeros_like(acc)

    h = h_ref[...]
    parts = []
    for k in range(u_ref.shape[0] // _EXPERT_SUB):
        sl = slice(k * _EXPERT_SUB, (k + 1) * _EXPERT_SUB)
        a = _gelu_tanh(_dot_nt(h, u_ref[sl, :]))
        parts.append((a * w_ref[:, sl].astype(f32)).astype(bf16))
    acc[...] += _dot(jnp.concatenate(parts, axis=1), v_ref[...])

    @pl.when(c == pl.num_programs(1) - 1)
    def _():
        o_ref[...] = x_ref[...] + g_ref[...] * acc[...]


def _peer_experts(h, u, v, li, w, x, g2, rows_per_cond, *, tt, ec):
    t, d = x.shape
    return pl.pallas_call(
        _expert_kernel,
        out_shape=jax.ShapeDtypeStruct((t, d), f32),
        grid=(t // tt, PEER_N // ec),
        in_specs=[
            pl.BlockSpec((tt, d), lambda i, c: (i, 0)),
            pl.BlockSpec((None, ec, d), lambda i, c: (li, c, 0)),
            pl.BlockSpec((None, ec, d), lambda i, c: (li, c, 0)),
            pl.BlockSpec((tt, ec), lambda i, c: (i, c)),
            pl.BlockSpec((tt, d), lambda i, c: (i, 0)),
            pl.BlockSpec((None, 1, d), lambda i, c: ((i * tt) // rows_per_cond, 0, 0)),
        ],
        out_specs=pl.BlockSpec((tt, d), lambda i, c: (i, 0)),
        scratch_shapes=[pltpu.VMEM((tt, d), f32)],
        compiler_params=_params("arbitrary", "arbitrary"),
        name="peer_experts",
    )(h, u, v, w, x, g2)


def _final_norm_kernel(x_ref, g_ref, o_ref):
    o_ref[...] = _rms(x_ref[...], g_ref[...])


def _final_norm(x, g, *, tm):
    t, d = x.shape
    return pl.pallas_call(
        _final_norm_kernel,
        out_shape=jax.ShapeDtypeStruct((t, d), f32),
        grid=(t // tm,),
        in_specs=[pl.BlockSpec((tm, d), lambda i: (i, 0)), pl.BlockSpec((1, d), lambda i: (0, 0))],
        out_specs=pl.BlockSpec((tm, d), lambda i: (i, 0)),
        compiler_params=_params("arbitrary"),
        name="final_norm",
    )(x, g.reshape(1, d))


def _rope_tables(l):
    t = jnp.arange(l)
    rows = (t // GRID_W).astype(f32)
    cols = (t % GRID_W).astype(f32)
    half = DA // 2
    freqs = ROPE_BASE ** (-jnp.arange(0, half, 2, dtype=f32) / half)
    lane = np.arange(LANES)
    use_col = (lane % DA) >= half
    upper = (lane % half) >= half // 2
    fr = freqs[lane % (half // 2)]
    ang = jnp.where(use_col[None, :], cols[:, None], rows[:, None]) * fr[None, :]
    cos, sin = jnp.cos(ang), jnp.sin(ang)
    sa = jnp.where(upper[None, :], sin, 0.0)
    sb = jnp.where(upper[None, :], 0.0, -sin)
    return cos, sa, sb


def _split_mod(m):
    return [m[:, None, i * D_MODEL:(i + 1) * D_MODEL] for i in range(6)]


def _trunk_layer(x, nb, l, mod, cache, prev_ctx, wts, lam_init, rope_tabs):
    (norm1_l, w_in_l, lam_l, dgn_l, decay_l, rgn_l, nab_l, wbr_l, wout_l, norm2_l,
     pwq_l, psk_l, pu_all, pv_all, li) = wts
    t = nb * l
    rows_per_cond = t if mod.shape[0] == 1 else l
    sh1, sc1, g1, sh2, sc2, g2 = _split_mod(mod)
    p = _norm_mod_matmul(x, norm1_l, sc1, sh1, w_in_l, rows_per_cond, tm=512)[0]
    p3 = p.reshape(nb, l, IN_TOTAL)
    if cache is None:
        pk = (lambda i, j: None) if prev_ctx is None else (lambda i, j: prev_ctx[i:j])
        ya, kd, vd = _diff_attn_ctx(p3, lam_l, dgn_l, lam_init, prev=pk(0, 2))
        yc, kn, vn = _dense_attn_ctx(p3, prev=pk(2, 4))
        yb, sr = _retention(p3, decay_l, rgn_l, tq=l, emit_state=True,
                            prev_state=None if prev_ctx is None else prev_ctx[4])
        new_ctx = (kd, vd, kn, vn, sr)
    else:
        c_dk, c_dv, c_nk, c_nv, c_st = cache
        ya = _diff_attn_lat(p3, c_dk, c_dv, rope_tabs, lam_l, dgn_l, lam_init, tq=256)
        yc = _na_attn_lat(p3, c_nk, c_nv, nab_l)
        yb = _retention(p3, decay_l, rgn_l, tq=256, rope_tabs=rope_tabs, s0=c_st)[0]
        new_ctx = None
    bw = HA * 2 * DA
    x = _merge_out(ya.reshape(t, bw), yb.reshape(t, bw), yc.reshape(t, bw), p, x, g1,
                   wbr_l, wout_l, rows_per_cond, tm=256)
    q, h2 = _norm_mod_matmul(x, norm2_l, sc2, sh2, pwq_l, rows_per_cond, tm=512, emit_h=True,
                             split_out=True)
    a, b, g = _peer_topk(q, psk_l, tt=512)
    w = _peer_weights(a, b, g, tw=128)
    x = _peer_experts(h2, pu_all, pv_all, li, w, x, g2, rows_per_cond, tt=1024, ec=2048)
    return x, new_ctx


def kernel(x_prompt, x_sample, c, cache_diff_k, cache_diff_v, cache_na_k, cache_na_v, state_ret,
           c_ctx, w_mod, b_mod, norm1, w_in, diff_lambda, diff_gn, ret_decay, ret_gn, na_bias,
           w_branch, w_out, norm2, peer_wq, peer_subkeys, peer_u, peer_v, norm_f):
    nbp, lp, d = x_prompt.shape
    nbs, ls, _ = x_sample.shape
    n_cond = SUBLANES
    conds = jnp.concatenate([c_ctx[None, :], c, jnp.zeros((n_cond - 1 - nbs, d), f32)], axis=0)
    mod = _modulation(conds, w_mod, b_mod)
    rope_tabs = _rope_tables(ls)

    pu_all = peer_u.astype(bf16)
    pv_all = peer_v.astype(bf16)
    wts = [(norm1[li], _column_slabs(w_in[li], 1024), diff_lambda[li], diff_gn[li], ret_decay[li],
            ret_gn[li], na_bias[li], w_branch[li].astype(bf16), w_out[li].astype(bf16), norm2[li],
            _column_slabs(peer_wq[li], 1024), peer_subkeys[li], pu_all, pv_all, li)
           for li in range(DEPTH)]

    xp = x_prompt.reshape(nbp * lp, d)
    ctx = None
    for li in range(DEPTH):
        lam_init = 0.8 - 0.6 * math.exp(-0.3 * li)
        xp, ctx = _trunk_layer(xp, nbp, lp, mod[li, 0:1], None, ctx, wts[li], lam_init, None)
    y_prompt = _final_norm(xp, norm_f, tm=512).reshape(nbp, lp, d)

    xs = x_sample.reshape(nbs * ls, d)
    for li in range(DEPTH):
        lam_init = 0.8 - 0.6 * math.exp(-0.3 * li)
        cache = (cache_diff_k[:, li], cache_diff_v[:, li], cache_na_k[:, li], cache_na_v[:, li],
                 state_ret[:, li])
        xs, _ = _trunk_layer(xs, nbs, ls, mod[li, 1:1 + nbs], cache, None, wts[li], lam_init,
                             rope_tabs)
    y_sample = _final_norm(xs, norm_f, tm=512).reshape(nbs, ls, d)
    return (y_prompt, y_sample, *ctx)
```
